```python
import jax, jax.numpy as jnp
from jax import lax
import numpy as np

D_MODEL = 1024
BATCH = 16
SEQ = 2048
DEPTH = 2

CHUNK = 64
N_MIXERS = 2
N_A_LAYERS = (DEPTH + N_MIXERS - 1) // N_MIXERS
N_B_LAYERS = DEPTH // N_MIXERS

GMLP_BLOCK = 128
GMLP_GROUPS = 8
GMLP_GROUP_DIM = D_MODEL // GMLP_GROUPS

N_HEADS = 16
HEAD_DIM = D_MODEL // N_HEADS
Q_BLOCK = 128

D_FF = -(-8 * D_MODEL // (3 * 256)) * 256

EPS = 1e-6

kernel_name = "hybrid_gmlp_fox_streaming_encoder"


def rms_norm(x, g):
    xf = x.astype(jnp.float32)
    y = xf * lax.rsqrt(jnp.mean(xf * xf, axis=-1, keepdims=True) + EPS)
    return (y * g.astype(jnp.float32)).astype(x.dtype)


def gmlp_mixer(h, w_in, ln_g, ln_b, w_s, b_s, w_out):
    bsz, seq, _ = h.shape
    z = jax.nn.gelu(h @ w_in, approximate=False)
    u, v = jnp.split(z, 2, axis=-1)
    vf = v.astype(jnp.float32)
    mu = jnp.mean(vf, axis=-1, keepdims=True)
    var = jnp.mean(jnp.square(vf - mu), axis=-1, keepdims=True)
    vn = ((vf - mu) * lax.rsqrt(var + EPS) * ln_g.astype(jnp.float32)
          + ln_b.astype(jnp.float32)).astype(h.dtype)
    nblk = seq // GMLP_BLOCK
    vn = vn.reshape(bsz, nblk, GMLP_BLOCK, GMLP_GROUPS, GMLP_GROUP_DIM)
    chunk_id = jnp.arange(GMLP_BLOCK) // CHUNK
    mask = chunk_id[None, :] <= chunk_id[:, None]
    ws = jnp.where(mask[None], w_s, 0)
    mixed = jnp.einsum('gts,bnsgc->bntgc', ws, vn) + b_s.T[None, None, :, :, None]
    gated = u * mixed.reshape(bsz, seq, D_MODEL).astype(u.dtype)
    return gated @ w_out


def fox_mixer(h, w_in, f_bias, q_g, k_g, w_out):
    bsz, seq, _ = h.shape
    proj = h @ w_in
    q, k, v, g, f_logit = jnp.split(
        proj, [D_MODEL, 2 * D_MODEL, 3 * D_MODEL, 4 * D_MODEL], axis=-1)
    q = rms_norm(q.reshape(bsz, seq, N_HEADS, HEAD_DIM), q_g)
    k = rms_norm(k.reshape(bsz, seq, N_HEADS, HEAD_DIM), k_g)
    v = v.reshape(bsz, seq, N_HEADS, HEAD_DIM)
    log_f = jax.nn.log_sigmoid((f_logit + f_bias).astype(jnp.float32))
    cum = jnp.cumsum(log_f, axis=1).transpose(0, 2, 1)
    scale = HEAD_DIM ** -0.5
    local = jnp.arange(Q_BLOCK)
    outs = []
    for i in range(seq // Q_BLOCK):
        q0 = i * Q_BLOCK
        kl = q0 + Q_BLOCK
        s = jnp.einsum('bqhd,bkhd->bhqk', q[:, q0:kl], k[:, :kl],
                       preferred_element_type=jnp.float32) * scale
        s = s + cum[:, :, q0:kl, None] - cum[:, :, None, :kl]
        causal = (q0 + local)[:, None] >= jnp.arange(kl)[None, :]
        s = jnp.where(causal[None, None], s, -jnp.inf)
        p = jax.nn.softmax(s, axis=-1).astype(v.dtype)
        outs.append(jnp.einsum('bhqk,bkhd->bqhd', p, v[:, :kl]))
    o = jnp.concatenate(outs, axis=1)
    o = o.reshape(bsz, seq, D_MODEL) * jax.nn.sigmoid(g)
    return o @ w_out


def swiglu(h, w_gate, w_up, w_down):
    return (jax.nn.silu(h @ w_gate) * (h @ w_up)) @ w_down


def setup_inputs(seed: int = 0) -> dict:
    key = jax.random.key(seed)
    ks = jax.random.split(key, 20)
    f32 = jnp.float32
    D, H, F = D_MODEL, N_HEADS, D_FF
    nrm = lambda k, shape: jax.random.normal(k, shape, f32)
    return {
        "x": nrm(ks[0], (BATCH, SEQ, D)),
        "norm_mix_g": 1.0 + 0.02 * nrm(ks[1], (DEPTH, D)),
        "norm_ffn_g": 1.0 + 0.02 * nrm(ks[2], (DEPTH, D)),
        "a_w_in": nrm(ks[3], (N_A_LAYERS, D, 2 * D)) * D ** -0.5,
        "a_ln_g": 1.0 + 0.02 * nrm(ks[4], (N_A_LAYERS, D)),
        "a_ln_b": 0.02 * nrm(ks[5], (N_A_LAYERS, D)),
        "a_w_s": nrm(ks[6], (N_A_LAYERS, GMLP_GROUPS, GMLP_BLOCK, GMLP_BLOCK)) * GMLP_BLOCK ** -0.5,
        "a_b_s": 1.0 + 0.1 * nrm(ks[7], (N_A_LAYERS, GMLP_GROUPS, GMLP_BLOCK)),
        "a_w_out": nrm(ks[8], (N_A_LAYERS, D, D)) * D ** -0.5,
        "b_w_in": nrm(ks[9], (N_B_LAYERS, D, 4 * D + H)) * D ** -0.5,
        "b_f_bias": 3.0 + 0.5 * nrm(ks[10], (N_B_LAYERS, H)),
        "b_q_norm_g": 1.0 + 0.02 * nrm(ks[11], (N_B_LAYERS, HEAD_DIM)),
        "b_k_norm_g": 1.0 + 0.02 * nrm(ks[12], (N_B_LAYERS, HEAD_DIM)),
        "b_w_out": nrm(ks[13], (N_B_LAYERS, D, D)) * D ** -0.5,
        "ffn_w_gate": nrm(ks[14], (DEPTH, D, F)) * D ** -0.5,
        "ffn_w_up": nrm(ks[15], (DEPTH, D, F)) * D ** -0.5,
        "ffn_w_down": nrm(ks[16], (DEPTH, F, D)) * F ** -0.5,
    }


def reference(x, norm_mix_g, norm_ffn_g, a_w_in, a_ln_g, a_ln_b, a_w_s, a_b_s, a_w_out,
              b_w_in, b_f_bias, b_q_norm_g, b_k_norm_g, b_w_out,
              ffn_w_gate, ffn_w_up, ffn_w_down):
    h = x
    for layer in range(DEPTH):
        j = layer // N_MIXERS
        hn = rms_norm(h, norm_mix_g[layer])
        if layer % N_MIXERS == 0:
            h = h + gmlp_mixer(hn, a_w_in[j], a_ln_g[j], a_ln_b[j], a_w_s[j], a_b_s[j], a_w_out[j])
        else:
            h = h + fox_mixer(hn, b_w_in[j], b_f_bias[j], b_q_norm_g[j], b_k_norm_g[j], b_w_out[j])
        h = h + swiglu(rms_norm(h, norm_ffn_g[layer]), ffn_w_gate[layer], ffn_w_up[layer], ffn_w_down[layer])
    return h
```

```python
import functools
import math

import jax
import jax.numpy as jnp
from jax import lax
from jax.experimental import pallas as pl
from jax.experimental.pallas import tpu as pltpu

F32 = jnp.float32
BF16 = jnp.bfloat16

EPS = 1e-6
CHUNK = 64
GMLP_BLOCK = 128
GMLP_GROUPS = 8
N_HEADS = 16
HEAD_DIM = 64
LANES = 128
HEADS_PER_TILE = LANES // HEAD_DIM

ROW_TILE = 256
Q_TILE = 256
VMEM_LIMIT_BYTES = 56 * 1024 * 1024


def _const_spec(shape):
    nd = len(shape)
    return pl.BlockSpec(shape, lambda *_: (0,) * nd, pipeline_mode=pl.Buffered(1))


def _params(semantics):
    return pltpu.CompilerParams(dimension_semantics=semantics,
                                vmem_limit_bytes=VMEM_LIMIT_BYTES)


def _rms_norm(x, g):
    ms = jnp.mean(x * x, axis=-1, keepdims=True)
    return x * lax.rsqrt(ms + EPS) * g


def _dot(a, b):
    return jnp.dot(a, b, preferred_element_type=F32)


def _gelu_exact(x):
    return 0.5 * x * (1.0 + lax.erf(x * math.sqrt(0.5)))


def _gmlp_kernel(x_ref, g_ref, win_ref, lng_ref, lnb_ref, ws_ref, bs_ref, wout_ref, o_ref):
    d = x_ref.shape[-1]
    x = x_ref[...]
    hn = _rms_norm(x, g_ref[...]).astype(BF16)
    z = _gelu_exact(_dot(hn, win_ref[...]))
    u = z[:, :d]
    v = z[:, d:]
    mu = jnp.mean(v, axis=-1, keepdims=True)
    vc = v - mu
    var = jnp.mean(vc * vc, axis=-1, keepdims=True)
    vn = (vc * lax.rsqrt(var + EPS) * lng_ref[...] + lnb_ref[...]).astype(BF16)

    t_chunk = lax.broadcasted_iota(jnp.int32, (GMLP_BLOCK, GMLP_BLOCK), 0) // CHUNK
    s_chunk = lax.broadcasted_iota(jnp.int32, (GMLP_BLOCK, GMLP_BLOCK), 1) // CHUNK
    visible = s_chunk <= t_chunk
    ws = [jnp.where(visible, ws_ref[g], 0.0).astype(BF16) for g in range(GMLP_GROUPS)]

    bias = bs_ref[...]
    blocks = []
    for n in range(x.shape[0] // GMLP_BLOCK):
        vb = vn[n * GMLP_BLOCK:(n + 1) * GMLP_BLOCK]
        cols = [_dot(ws[g], vb[:, g * LANES:(g + 1) * LANES]) for g in range(GMLP_GROUPS)]
        blocks.append(jnp.concatenate(cols, axis=1) + bias)
    mixed = jnp.concatenate(blocks, axis=0)
    gated = (u * mixed).astype(BF16)
    o_ref[...] = x + _dot(gated, wout_ref[...])


def _gmlp_layer(x2d, g, w_in, ln_g, ln_b, w_s, b_s_tile, w_out):
    t, d = x2d.shape
    row = pl.BlockSpec((ROW_TILE, d), lambda i: (i, 0))
    return pl.pallas_call(
        _gmlp_kernel,
        grid=(t // ROW_TILE,),
        in_specs=[row, _const_spec(g.shape), _const_spec(w_in.shape), _const_spec(ln_g.shape),
                  _const_spec(ln_b.shape), _const_spec(w_s.shape), _const_spec(b_s_tile.shape),
                  _const_spec(w_out.shape)],
        out_specs=row,
        out_shape=jax.ShapeDtypeStruct((t, d), F32),
        compiler_params=_params(("parallel",)),
        name="gmlp_mixer",
    )(x2d, g, w_in, ln_g, ln_b, w_s, b_s_tile, w_out)


def _swiglu(h, g, wg_ref, wu_ref, wd_ref):
    hn = _rms_norm(h, g).astype(BF16)
    a = _dot(hn, wg_ref[...])
    b = _dot(hn, wu_ref[...])
    act = (a * jax.nn.sigmoid(a) * b).astype(BF16)
    return h + _dot(act, wd_ref[...])


def _ffn_kernel(h_ref, g_ref, wg_ref, wu_ref, wd_ref, o_ref):
    o_ref[...] = _swiglu(h_ref[...], g_ref[...], wg_ref, wu_ref, wd_ref)


def _proj_ffn_kernel(h_ref, a_ref, wo_ref, g_ref, wg_ref, wu_ref, wd_ref, o_ref):
    h = h_ref[...] + _dot(a_ref[...], wo_ref[...])
    o_ref[...] = _swiglu(h, g_ref[...], wg_ref, wu_ref, wd_ref)


def _ffn_layer(h2d, g, w_gate, w_up, w_down, attn=None, w_o=None):
    t, d = h2d.shape
    row = pl.BlockSpec((ROW_TILE, d), lambda i: (i, 0))
    weights = [g, w_gate, w_up, w_down]
    if attn is None:
        kernel, args, specs, name = _ffn_kernel, [h2d], [row], "swiglu_ffn"
    else:
        kernel, args, specs, name = _proj_ffn_kernel, [h2d, attn, w_o], \
            [row, row, _const_spec(w_o.shape)], "attn_out_swiglu_ffn"
    return pl.pallas_call(
        kernel,
        grid=(t // ROW_TILE,),
        in_specs=specs + [_const_spec(w.shape) for w in weights],
        out_specs=row,
        out_shape=jax.ShapeDtypeStruct((t, d), F32),
        compiler_params=_params(("parallel",)),
        name=name,
    )(*args, *weights)


def _split3(x):
    hi = x.astype(BF16)
    r = x - hi.astype(F32)
    mid = r.astype(BF16)
    lo = (r - mid.astype(F32)).astype(BF16)
    return hi, mid, lo


def _fox_proj_kernel(h_ref, g_ref, wq_ref, wk_ref, wv_ref, wg_ref, wf_ref, fb_ref,
                     q_ref, k_ref, v_ref, gate_ref, cum_ref, cumt_ref, carry_ref):
    @pl.when(pl.program_id(1) == 0)
    def _():
        carry_ref[...] = jnp.zeros_like(carry_ref)

    hn = _rms_norm(h_ref[0], g_ref[...]).astype(BF16)
    q_ref[0] = _dot(hn, wq_ref[...]).astype(BF16)
    k_ref[0] = _dot(hn, wk_ref[...]).astype(BF16)
    v_ref[0] = _dot(hn, wv_ref[...]).astype(BF16)
    gate_ref[0] = jax.nn.sigmoid(_dot(hn, wg_ref[...])).astype(BF16)

    log_f = jax.nn.log_sigmoid(_dot(hn, wf_ref[...]) + fb_ref[...])
    tm = log_f.shape[0]
    lower = (lax.broadcasted_iota(jnp.int32, (tm, tm), 0)
             >= lax.broadcasted_iota(jnp.int32, (tm, tm), 1)).astype(BF16)
    hi, mid, lo = _split3(log_f)
    cum = carry_ref[...] + (_dot(lower, hi) + _dot(lower, mid) + _dot(lower, lo))
    carry_ref[...] = cum[tm - 1:tm, :]
    cum_ref[0] = cum
    cumt_ref[0] = cum.T[:N_HEADS, :]


def _fox_proj(h3d, g, wq, wk, wv, wg, wf_pad, fb_pad):
    b, s, d = h3d.shape
    row = pl.BlockSpec((1, ROW_TILE, d), lambda i, j: (i, j, 0))
    bf = jax.ShapeDtypeStruct((b, s, d), BF16)
    return pl.pallas_call(
        _fox_proj_kernel,
        grid=(b, s // ROW_TILE),
        in_specs=[row] + [_const_spec(w.shape) for w in (g, wq, wk, wv, wg, wf_pad, fb_pad)],
        out_specs=[row, row, row, row,
                   pl.BlockSpec((1, ROW_TILE, LANES), lambda i, j: (i, j, 0)),
                   pl.BlockSpec((1, N_HEADS, ROW_TILE), lambda i, j: (i, 0, j))],
        out_shape=[bf, bf, bf, bf,
                   jax.ShapeDtypeStruct((b, s, LANES), F32),
                   jax.ShapeDtypeStruct((b, N_HEADS, s), F32)],
        scratch_shapes=[pltpu.VMEM((1, LANES), F32)],
        compiler_params=_params(("parallel", "arbitrary")),
        name="fox_proj",
    )(h3d, g, wq, wk, wv, wg, wf_pad, fb_pad)


def _head_norm(x_ref, gain, first_half):
    x = x_ref[0].astype(F32)
    sq = x * x
    ss_a = jnp.sum(jnp.where(first_half, sq, 0.0), axis=-1, keepdims=True)
    ss_b = jnp.sum(jnp.where(first_half, 0.0, sq), axis=-1, keepdims=True)
    inv = lax.rsqrt(jnp.where(first_half, ss_a, ss_b) * (1.0 / HEAD_DIM) + EPS)
    return x * inv * gain


def _fox_attn_kernel(q_ref, k_ref, v_ref, gate_ref, cum_ref, cumt_ref, qg_ref, kg_ref, o_ref):
    s_len = q_ref.shape[1]
    pair = pl.program_id(1)
    lane = lax.broadcasted_iota(jnp.int32, (1, LANES), 1)
    first_half = lane < HEAD_DIM

    qn = _head_norm(q_ref, qg_ref[...] * (HEAD_DIM ** -0.5), first_half)
    kn = _head_norm(k_ref, kg_ref[...], first_half).astype(BF16)
    q_heads = [jnp.where(first_half, qn, 0.0).astype(BF16),
               jnp.where(first_half, 0.0, qn).astype(BF16)]
    v = v_ref[0]
    cum = cum_ref[0]

    row_id = lax.broadcasted_iota(jnp.int32, (Q_TILE, Q_TILE), 0)
    col_id = lax.broadcasted_iota(jnp.int32, (Q_TILE, Q_TILE), 1)
    causal = row_id >= col_id

    cq = []
    ck = []
    for j in range(HEADS_PER_TILE):
        head = pair * HEADS_PER_TILE + j
        cq.append(jnp.sum(jnp.where(lane == head, cum, 0.0), axis=-1, keepdims=True))
        ck.append(cumt_ref[0, pl.ds(head, 1), :])

    nt = (((1,), (1,)), ((), ()))
    for i in range(s_len // Q_TILE):
        q0 = i * Q_TILE
        kl = q0 + Q_TILE
        outs = []
        for j in range(HEADS_PER_TILE):
            qb = q_heads[j][q0:kl]
            logits = lax.dot_general(qb, kn[:kl], nt, preferred_element_type=F32)
            logits = logits + cq[j][q0:kl] - ck[j][:, :kl]
            diag = jnp.where(causal, logits[:, q0:kl], -jnp.inf)
            m = jnp.max(diag, axis=-1, keepdims=True)
            if i > 0:
                past = logits[:, :q0]
                m = jnp.maximum(m, jnp.max(past, axis=-1, keepdims=True))
            p_diag = jnp.exp(diag - m)
            l = jnp.sum(p_diag, axis=-1, keepdims=True)
            acc = _dot(p_diag.astype(BF16), v[q0:kl])
            if i > 0:
                p_past = jnp.exp(past - m)
                l = l + jnp.sum(p_past, axis=-1, keepdims=True)
                acc = acc + _dot(p_past.astype(BF16), v[:q0])
            outs.append(acc / l)
        o = jnp.where(first_half, outs[0], outs[1])
        o_ref[0, q0:kl, :] = (o * gate_ref[0, q0:kl, :].astype(F32)).astype(BF16)


def _fox_attn(q, k, v, gate, cum, cumt, qg_tile, kg_tile):
    b, s, d = q.shape
    tile = pl.BlockSpec((1, s, LANES), lambda i, j: (i, 0, j))
    return pl.pallas_call(
        _fox_attn_kernel,
        grid=(b, d // LANES),
        in_specs=[tile, tile, tile, tile,
                  pl.BlockSpec((1, s, LANES), lambda i, j: (i, 0, 0)),
                  pl.BlockSpec((1, N_HEADS, s), lambda i, j: (i, 0, 0)),
                  _const_spec(qg_tile.shape), _const_spec(kg_tile.shape)],
        out_specs=tile,
        out_shape=jax.ShapeDtypeStruct((b, s, d), BF16),
        compiler_params=_params(("parallel", "parallel")),
        name="fox_attention",
    )(q, k, v, gate, cum, cumt, qg_tile, kg_tile)


def kernel(x, norm_mix_g, norm_ffn_g, a_w_in, a_ln_g, a_ln_b, a_w_s, a_b_s, a_w_out,
           b_w_in, b_f_bias, b_q_norm_g, b_k_norm_g, b_w_out,
           ffn_w_gate, ffn_w_up, ffn_w_down):
    bsz, seq, d = x.shape
    row = lambda p: p.reshape(1, -1).astype(F32)
    bf = lambda w: w.astype(BF16)

    b_s_tile = jnp.repeat(a_b_s[0].T, LANES, axis=1)
    h = _gmlp_layer(x.reshape(bsz * seq, d), row(norm_mix_g[0]), bf(a_w_in[0]),
                    row(a_ln_g[0]), row(a_ln_b[0]), a_w_s[0], b_s_tile, bf(a_w_out[0]))
    h = _ffn_layer(h, row(norm_ffn_g[0]), bf(ffn_w_gate[0]), bf(ffn_w_up[0]), bf(ffn_w_down[0]))

    w_in = b_w_in[0]
    wq, wk, wv, wg = (bf(w_in[:, i * d:(i + 1) * d]) for i in range(4))
    wf_pad = bf(jnp.pad(w_in[:, 4 * d:], ((0, 0), (0, LANES - N_HEADS))))
    fb_pad = jnp.pad(b_f_bias[0], (0, LANES - N_HEADS)).reshape(1, LANES)
    q, k, v, gate, cum, cumt = _fox_proj(h.reshape(bsz, seq, d), row(norm_mix_g[1]),
                                         wq, wk, wv, wg, wf_pad, fb_pad)
    qg_tile = jnp.tile(b_q_norm_g[0], HEADS_PER_TILE).reshape(1, LANES)
    kg_tile = jnp.tile(b_k_norm_g[0], HEADS_PER_TILE).reshape(1, LANES)
    attn = _fox_attn(q, k, v, gate, cum, cumt, qg_tile, kg_tile)
    h = _ffn_layer(h, row(norm_ffn_g[1]), bf(ffn_w_gate[1]), bf(ffn_w_up[1]), bf(ffn_w_down[1]),
                   attn=attn.reshape(bsz * seq, d), w_o=bf(b_w_out[0]))
    return h.reshape(bsz, seq, d)
```

```python
import math

import jax
import jax.numpy as jnp
from jax import lax
from jax.experimental import pallas as pl
from jax.experimental.pallas import tpu as pltpu

F32 = jnp.float32
BF16 = jnp.bfloat16

EPS = 1e-6
CHUNK = 64
GMLP_BLOCK = 128
GMLP_GROUPS = 8
N_HEADS = 16
HEAD_DIM = 64
LANES = 128
SUBLANES = 8
BF16_SUBLANES = 16
HEADS_PER_TILE = LANES // HEAD_DIM
LOG2E = math.log2(math.e)

ROW_TILE = 256
Q_TILE = 256
VMEM_LIMIT_BYTES = 56 * 1024 * 1024


def _const_spec(shape):
    nd = len(shape)
    return pl.BlockSpec(shape, lambda *_: (0,) * nd, pipeline_mode=pl.Buffered(1))


def _params(semantics):
    return pltpu.CompilerParams(dimension_semantics=semantics,
                                vmem_limit_bytes=VMEM_LIMIT_BYTES)


def _rms_norm(x, g):
    ms = jnp.mean(x * x, axis=-1, keepdims=True)
    return x * lax.rsqrt(ms + EPS) * g


def _dot(a, b):
    return jnp.dot(a, b, preferred_element_type=F32)


def _gelu_exact(x):
    return 0.5 * x * (1.0 + lax.erf(x * math.sqrt(0.5)))


def _split3(x):
    hi = x.astype(BF16).astype(F32)
    r = x - hi
    mid = r.astype(BF16).astype(F32)
    lo = (r - mid).astype(BF16).astype(F32)
    return hi, mid, lo


def _gmlp_kernel(x_ref, g_ref, win_ref, lng_ref, lnb_ref, ws_ref, bs_ref, wout_ref, o_ref):
    d = x_ref.shape[-1]
    x = x_ref[...]
    hn = _rms_norm(x, g_ref[...]).astype(BF16)
    z = _gelu_exact(_dot(hn, win_ref[...]))
    u = z[:, :d]
    v = z[:, d:]
    mu = jnp.mean(v, axis=-1, keepdims=True)
    vc = v - mu
    var = jnp.mean(vc * vc, axis=-1, keepdims=True)
    vn = (vc * lax.rsqrt(var + EPS) * lng_ref[...] + lnb_ref[...]).astype(BF16)

    t_chunk = lax.broadcasted_iota(jnp.int32, (GMLP_BLOCK, GMLP_BLOCK), 0) // CHUNK
    s_chunk = lax.broadcasted_iota(jnp.int32, (GMLP_BLOCK, GMLP_BLOCK), 1) // CHUNK
    visible = s_chunk <= t_chunk
    ws = [jnp.where(visible, ws_ref[g], 0.0).astype(BF16) for g in range(GMLP_GROUPS)]

    bias = bs_ref[...]
    blocks = []
    for n in range(x.shape[0] // GMLP_BLOCK):
        vb = vn[n * GMLP_BLOCK:(n + 1) * GMLP_BLOCK]
        cols = [_dot(ws[g], vb[:, g * LANES:(g + 1) * LANES]) for g in range(GMLP_GROUPS)]
        blocks.append(jnp.concatenate(cols, axis=1) + bias)
    mixed = jnp.concatenate(blocks, axis=0)
    gated = (u * mixed).astype(BF16)
    o_ref[...] = x + _dot(gated, wout_ref[...])


def _gmlp_layer(x2d, g, w_in, ln_g, ln_b, w_s, b_s_tile, w_out):
    t, d = x2d.shape
    row = pl.BlockSpec((ROW_TILE, d), lambda i: (i, 0))
    return pl.pallas_call(
        _gmlp_kernel,
        grid=(t // ROW_TILE,),
        in_specs=[row, _const_spec(g.shape), _const_spec(w_in.shape), _const_spec(ln_g.shape),
                  _const_spec(ln_b.shape), _const_spec(w_s.shape), _const_spec(b_s_tile.shape),
                  _const_spec(w_out.shape)],
        out_specs=row,
        out_shape=jax.ShapeDtypeStruct((t, d), F32),
        compiler_params=_params(("parallel",)),
        name="gmlp_mixer",
    )(x2d, g, w_in, ln_g, ln_b, w_s, b_s_tile, w_out)


def _swiglu(h, g, wg_ref, wu_ref, wd_ref):
    hn = _rms_norm(h, g).astype(BF16)
    a = _dot(hn, wg_ref[...])
    b = _dot(hn, wu_ref[...])
    act = (a * jax.nn.sigmoid(a) * b).astype(BF16)
    return h + _dot(act, wd_ref[...])


def _ffn_kernel(h_ref, g_ref, wg_ref, wu_ref, wd_ref, o_ref):
    o_ref[...] = _swiglu(h_ref[...], g_ref[...], wg_ref, wu_ref, wd_ref)


def _proj_ffn_kernel(h_ref, a_ref, wo_ref, g_ref, wg_ref, wu_ref, wd_ref, o_ref):
    a = jnp.concatenate([a_ref[0, p] for p in range(a_ref.shape[1])], axis=1)
    h = h_ref[...] + _dot(a, wo_ref[...])
    o_ref[...] = _swiglu(h, g_ref[...], wg_ref, wu_ref, wd_ref)


def _ffn_layer(h2d, g, w_gate, w_up, w_down, attn=None, w_o=None):
    t, d = h2d.shape
    row = pl.BlockSpec((ROW_TILE, d), lambda i: (i, 0))
    weights = [g, w_gate, w_up, w_down]
    if attn is None:
        kernel, args, specs, name = _ffn_kernel, [h2d], [row], "swiglu_ffn"
    else:
        tiles_per_seq = attn.shape[2] // ROW_TILE
        attn_spec = pl.BlockSpec((1, attn.shape[1], ROW_TILE, LANES),
                                 lambda i: (i // tiles_per_seq, 0, i % tiles_per_seq, 0))
        kernel, args, specs, name = _proj_ffn_kernel, [h2d, attn, w_o], \
            [row, attn_spec, _const_spec(w_o.shape)], "attn_out_swiglu_ffn"
    return pl.pallas_call(
        kernel,
        grid=(t // ROW_TILE,),
        in_specs=specs + [_const_spec(w.shape) for w in weights],
        out_specs=row,
        out_shape=jax.ShapeDtypeStruct((t, d), F32),
        compiler_params=_params(("parallel",)),
        name=name,
    )(*args, *weights)


def _store_head_pairs(ref, x):
    for p in range(ref.shape[1]):
        ref[0, p] = x[:, p * LANES:(p + 1) * LANES].astype(ref.dtype)


def _fox_proj_kernel(h_ref, g_ref, wq_ref, wk_ref, wv_ref, wg_ref, wf_ref, fb_ref,
                     q_ref, k_ref, vt_ref, gate_ref, cumt_ref, carry_ref):
    @pl.when(pl.program_id(1) == 0)
    def _():
        carry_ref[...] = jnp.zeros_like(carry_ref)

    hn = _rms_norm(h_ref[0], g_ref[...]).astype(BF16)
    _store_head_pairs(q_ref, _dot(hn, wq_ref[...]))
    _store_head_pairs(k_ref, _dot(hn, wk_ref[...]))
    vt_ref[0] = _dot(hn, wv_ref[...]).T.astype(BF16)
    _store_head_pairs(gate_ref, jax.nn.sigmoid(_dot(hn, wg_ref[...])))

    log_f = jax.nn.log_sigmoid(_dot(hn, wf_ref[...]) + fb_ref[...])
    tm = log_f.shape[0]
    lower = (lax.broadcasted_iota(jnp.int32, (tm, tm), 0)
             >= lax.broadcasted_iota(jnp.int32, (tm, tm), 1)).astype(BF16)
    hi, mid, lo = _split3(log_f)
    cum = carry_ref[...] + (_dot(lower, hi.astype(BF16)) + _dot(lower, mid.astype(BF16))
                            + _dot(lower, lo.astype(BF16)))
    carry_ref[...] = cum[tm - 1:tm, :]
    cumt_ref[0] = cum.T[:N_HEADS, :]


def _fox_proj(h3d, g, wq, wk, wv, wg, wf_pad, fb_pad):
    b, s, d = h3d.shape
    row = pl.BlockSpec((1, ROW_TILE, d), lambda i, j: (i, j, 0))
    pairs = pl.BlockSpec((1, d // LANES, ROW_TILE, LANES), lambda i, j: (i, 0, j, 0))
    bf = jax.ShapeDtypeStruct((b, d // LANES, s, LANES), BF16)
    return pl.pallas_call(
        _fox_proj_kernel,
        grid=(b, s // ROW_TILE),
        in_specs=[row] + [_const_spec(w.shape) for w in (g, wq, wk, wv, wg, wf_pad, fb_pad)],
        out_specs=[pairs, pairs, pl.BlockSpec((1, d, ROW_TILE), lambda i, j: (i, 0, j)), pairs,
                   pl.BlockSpec((1, N_HEADS, ROW_TILE), lambda i, j: (i, 0, j))],
        out_shape=[bf, bf, jax.ShapeDtypeStruct((b, d, s), BF16), bf,
                   jax.ShapeDtypeStruct((b, N_HEADS, s), F32)],
        scratch_shapes=[pltpu.VMEM((1, LANES), F32)],
        compiler_params=_params(("parallel", "arbitrary")),
        name="fox_proj",
    )(h3d, g, wq, wk, wv, wg, wf_pad, fb_pad)


def _head_norm(x, gain, first_half):
    x = x.astype(F32)
    sq = x * x
    ss_a = jnp.sum(jnp.where(first_half, sq, 0.0), axis=-1, keepdims=True)
    ss_b = jnp.sum(jnp.where(first_half, 0.0, sq), axis=-1, keepdims=True)
    inv = lax.rsqrt(jnp.where(first_half, ss_a, ss_b) * (1.0 / HEAD_DIM) + EPS)
    return x * inv * gain


def _decay_lanes(c_row, lane_base):
    n = c_row.shape[1]
    hi, mid, lo = _split3(c_row)
    sub = lax.broadcasted_iota(jnp.int32, (SUBLANES, n), 0)
    rows = jnp.where((sub == 0) | (sub == 3), hi,
                     jnp.where((sub == 1) | (sub == 4), mid,
                               jnp.where((sub == 2) | (sub == 5), lo, 0.0)))
    parts = []
    if lane_base:
        parts.append(jnp.zeros((lane_base, n), F32))
    parts.append(rows)
    parts.append(jnp.zeros((LANES - lane_base - SUBLANES, n), F32))
    return jnp.concatenate(parts, axis=0).T


def _fox_attn_kernel(q_ref, k_ref, vt_ref, gate_ref, cumt_ref, qg_ref, kg_ref, o_ref):
    s_len = q_ref.shape[2]
    pair = pl.program_id(1)
    lane = lax.broadcasted_iota(jnp.int32, (1, LANES), 1)
    first_half = lane < HEAD_DIM
    q_gain = qg_ref[...] * (HEAD_DIM ** -0.5 * LOG2E)
    k_gain = kg_ref[...]
    visible = (lax.broadcasted_iota(jnp.int32, (Q_TILE, Q_TILE), 0)
               <= lax.broadcasted_iota(jnp.int32, (Q_TILE, Q_TILE), 1))
    nt = (((1,), (1,)), ((), ()))

    k_chunks = [[] for _ in range(HEADS_PER_TILE)]

    def scores(i):
        q0, q1 = i * Q_TILE, (i + 1) * Q_TILE
        qn = _head_norm(q_ref[0, 0, q0:q1, :], q_gain, first_half)
        kn = _head_norm(k_ref[0, 0, q0:q1, :], k_gain, first_half)
        sts = []
        for j in range(HEADS_PER_TILE):
            own = first_half if j == 0 else jnp.logical_not(first_half)
            base = HEAD_DIM * (1 - j)
            c_row = cumt_ref[0, pl.ds(pair * HEADS_PER_TILE + j, 1), q0:q1] * LOG2E
            dec = _decay_lanes(c_row, base)
            q_side = (lane >= base) & (lane < base + 3)
            k_side = (lane >= base + 3) & (lane < base + 6)
            qh = jnp.where(own, qn, jnp.where(k_side, 1.0, dec)).astype(BF16)
            k_chunks[j].append(jnp.where(own, kn, jnp.where(q_side, 1.0, -dec)).astype(BF16))
            k_all = jnp.concatenate(k_chunks[j], axis=0)
            sts.append(lax.dot_general(k_all, qh, nt, preferred_element_type=F32))
        return sts

    def finish(i, sts):
        q0, q1 = i * Q_TILE, (i + 1) * Q_TILE
        ones_rows = jnp.ones((BF16_SUBLANES, q1), BF16)
        outs = []
        for j, st in enumerate(sts):
            diag = jnp.where(visible, st[q0:], -jnp.inf)
            m = jnp.max(diag, axis=0, keepdims=True)
            if i:
                past = st[:q0]
                m = jnp.maximum(m, jnp.max(past, axis=0, keepdims=True))
                pt = jnp.concatenate([jnp.exp2(past - m).astype(BF16),
                                      jnp.exp2(diag - m).astype(BF16)], axis=0)
            else:
                pt = jnp.exp2(diag - m).astype(BF16)
            vt = jnp.concatenate([vt_ref[0, j * HEAD_DIM:(j + 1) * HEAD_DIM, :q1], ones_rows], axis=0)
            acc = _dot(vt, pt)
            outs.append(acc[:HEAD_DIM] / acc[HEAD_DIM:HEAD_DIM + 1])
        o = jnp.concatenate(outs, axis=0).T
        o_ref[0, 0, q0:q1, :] = (o * gate_ref[0, 0, q0:q1, :].astype(F32)).astype(BF16)

    n_tiles = s_len // Q_TILE
    pending = scores(0)
    for i in range(n_tiles):
        upcoming = scores(i + 1) if i + 1 < n_tiles else None
        finish(i, pending)
        pending = upcoming


def _fox_attn(q, k, vt, gate, cumt, qg_tile, kg_tile):
    b, n_pairs, s, _ = q.shape
    tile = pl.BlockSpec((1, 1, s, LANES), lambda i, j: (i, j, 0, 0))
    return pl.pallas_call(
        _fox_attn_kernel,
        grid=(b, n_pairs),
        in_specs=[tile, tile, pl.BlockSpec((1, LANES, s), lambda i, j: (i, j, 0)), tile,
                  pl.BlockSpec((1, N_HEADS, s), lambda i, j: (i, 0, 0)),
                  _const_spec(qg_tile.shape), _const_spec(kg_tile.shape)],
        out_specs=tile,
        out_shape=jax.ShapeDtypeStruct(q.shape, BF16),
        compiler_params=_params(("parallel", "parallel")),
        name="fox_attention",
    )(q, k, vt, gate, cumt, qg_tile, kg_tile)


def kernel(x, norm_mix_g, norm_ffn_g, a_w_in, a_ln_g, a_ln_b, a_w_s, a_b_s, a_w_out,
           b_w_in, b_f_bias, b_q_norm_g, b_k_norm_g, b_w_out,
           ffn_w_gate, ffn_w_up, ffn_w_down):
    bsz, seq, d = x.shape
    row = lambda p: p.reshape(1, -1).astype(F32)
    bf = lambda w: w.astype(BF16)

    b_s_tile = jnp.repeat(a_b_s[0].T, LANES, axis=1)
    h = _gmlp_layer(x.reshape(bsz * seq, d), row(norm_mix_g[0]), bf(a_w_in[0]),
                    row(a_ln_g[0]), row(a_ln_b[0]), a_w_s[0], b_s_tile, bf(a_w_out[0]))
    h = _ffn_layer(h, row(norm_ffn_g[0]), bf(ffn_w_gate[0]), bf(ffn_w_up[0]), bf(ffn_w_down[0]))

    w_in = b_w_in[0]
    wq, wk, wv, wg = (bf(w_in[:, i * d:(i + 1) * d]) for i in range(4))
    wf_pad = bf(jnp.pad(w_in[:, 4 * d:], ((0, 0), (0, LANES - N_HEADS))))
    fb_pad = jnp.pad(b_f_bias[0], (0, LANES - N_HEADS)).reshape(1, LANES)
    q, k, vt, gate, cumt = _fox_proj(h.reshape(bsz, seq, d), row(norm_mix_g[1]),
                                     wq, wk, wv, wg, wf_pad, fb_pad)
    qg_tile = jnp.tile(b_q_norm_g[0], HEADS_PER_TILE).reshape(1, LANES)
    kg_tile = jnp.tile(b_k_norm_g[0], HEADS_PER_TILE).reshape(1, LANES)
    attn = _fox_attn(q, k, vt, gate, cumt, qg_tile, kg_tile)
    h = _ffn_layer(h, row(norm_ffn_g[1]), bf(ffn_w_gate[1]), bf(ffn_w_up[1]), bf(ffn_w_down[1]),
                   attn=attn, w_o=bf(b_w_out[0]))
    return h.reshape(bsz, seq, d)
```

```python
import math

import jax
import jax.numpy as jnp
from jax import lax
from jax.experimental import pallas as pl
from jax.experimental.pallas import tpu as pltpu

F32 = jnp.float32
BF16 = jnp.bfloat16

EPS = 1e-6
CHUNK = 64
GMLP_BLOCK = 128
GMLP_GROUPS = 8
N_HEADS = 16
HEAD_DIM = 64
LANES = 128
SUBLANES = 8
BF16_SUBLANES = 16
HEADS_PER_TILE = LANES // HEAD_DIM
LOG2E = math.log2(math.e)

ROW_TILE = 256
FFN_ROW_TILE = 512
Q_TILE = 256
SCORE_LOOKAHEAD = 4
VMEM_LIMIT_BYTES = 56 * 1024 * 1024


def _const_spec(shape):
    nd = len(shape)
    return pl.BlockSpec(shape, lambda *_: (0,) * nd, pipeline_mode=pl.Buffered(1))


def _params(semantics):
    return pltpu.CompilerParams(dimension_semantics=semantics,
                                vmem_limit_bytes=VMEM_LIMIT_BYTES)


def _rms_norm(x, g):
    ms = jnp.mean(x * x, axis=-1, keepdims=True)
    return x * lax.rsqrt(ms + EPS) * g


def _dot(a, b):
    return jnp.dot(a, b, preferred_element_type=F32)


def _gelu_exact(x):
    return 0.5 * x * (1.0 + lax.erf(x * math.sqrt(0.5)))


def _split3(x):
    hi = x.astype(BF16).astype(F32)
    r = x - hi
    mid = r.astype(BF16).astype(F32)
    lo = (r - mid).astype(BF16).astype(F32)
    return hi, mid, lo


def _gmlp_kernel(x_ref, g_ref, win_ref, lng_ref, lnb_ref, ws_ref, bs_ref, wout_ref, o_ref):
    d = x_ref.shape[-1]
    x = x_ref[...]
    hn = _rms_norm(x, g_ref[...]).astype(BF16)
    z = _gelu_exact(_dot(hn, win_ref[...]))
    u = z[:, :d]
    v = z[:, d:]
    mu = jnp.mean(v, axis=-1, keepdims=True)
    vc = v - mu
    var = jnp.mean(vc * vc, axis=-1, keepdims=True)
    vn = (vc * lax.rsqrt(var + EPS) * lng_ref[...] + lnb_ref[...]).astype(BF16)

    t_chunk = lax.broadcasted_iota(jnp.int32, (GMLP_BLOCK, GMLP_BLOCK), 0) // CHUNK
    s_chunk = lax.broadcasted_iota(jnp.int32, (GMLP_BLOCK, GMLP_BLOCK), 1) // CHUNK
    visible = s_chunk <= t_chunk
    ws = [jnp.where(visible, ws_ref[g], 0.0).astype(BF16) for g in range(GMLP_GROUPS)]

    bias = bs_ref[...]
    blocks = []
    for n in range(x.shape[0] // GMLP_BLOCK):
        vb = vn[n * GMLP_BLOCK:(n + 1) * GMLP_BLOCK]
        cols = [_dot(ws[g], vb[:, g * LANES:(g + 1) * LANES]) for g in range(GMLP_GROUPS)]
        blocks.append(jnp.concatenate(cols, axis=1) + bias)
    mixed = jnp.concatenate(blocks, axis=0)
    gated = (u * mixed).astype(BF16)
    o_ref[...] = x + _dot(gated, wout_ref[...])


def _gmlp_layer(x2d, g, w_in, ln_g, ln_b, w_s, b_s_tile, w_out):
    t, d = x2d.shape
    row = pl.BlockSpec((ROW_TILE, d), lambda i: (i, 0))
    return pl.pallas_call(
        _gmlp_kernel,
        grid=(t // ROW_TILE,),
        in_specs=[row, _const_spec(g.shape), _const_spec(w_in.shape), _const_spec(ln_g.shape),
                  _const_spec(ln_b.shape), _const_spec(w_s.shape), _const_spec(b_s_tile.shape),
                  _const_spec(w_out.shape)],
        out_specs=row,
        out_shape=jax.ShapeDtypeStruct((t, d), F32),
        compiler_params=_params(("parallel",)),
        name="gmlp_mixer",
    )(x2d, g, w_in, ln_g, ln_b, w_s, b_s_tile, w_out)


def _swiglu(h, g, wg_ref, wu_ref, wd_ref):
    hn = _rms_norm(h, g).astype(BF16)
    a = _dot(hn, wg_ref[...])
    b = _dot(hn, wu_ref[...])
    act = (a * jax.nn.sigmoid(a) * b).astype(BF16)
    return h + _dot(act, wd_ref[...])


def _ffn_kernel(h_ref, g_ref, wg_ref, wu_ref, wd_ref, o_ref):
    o_ref[...] = _swiglu(h_ref[...], g_ref[...], wg_ref, wu_ref, wd_ref)


def _load_head_pairs(ref):
    return jnp.concatenate([ref[0, p] for p in range(ref.shape[1])], axis=1).astype(F32)


def _proj_ffn_kernel(h_ref, a_ref, gate_ref, wo_ref, g_ref, wg_ref, wu_ref, wd_ref, o_ref):
    a = (_load_head_pairs(a_ref) * _load_head_pairs(gate_ref)).astype(BF16)
    h = h_ref[...] + _dot(a, wo_ref[...])
    o_ref[...] = _swiglu(h, g_ref[...], wg_ref, wu_ref, wd_ref)


def _ffn_layer(h2d, g, w_gate, w_up, w_down, attn=None, gate=None, w_o=None):
    t, d = h2d.shape
    tm = FFN_ROW_TILE
    row = pl.BlockSpec((tm, d), lambda i: (i, 0))
    weights = [g, w_gate, w_up, w_down]
    if attn is None:
        kernel, args, specs, name = _ffn_kernel, [h2d], [row], "swiglu_ffn"
    else:
        tiles_per_seq = attn.shape[2] // tm
        pairs = pl.BlockSpec((1, attn.shape[1], tm, LANES),
                             lambda i: (i // tiles_per_seq, 0, i % tiles_per_seq, 0))
        kernel, args, specs, name = _proj_ffn_kernel, [h2d, attn, gate, w_o], \
            [row, pairs, pairs, _const_spec(w_o.shape)], "attn_out_swiglu_ffn"
    return pl.pallas_call(
        kernel,
        grid=(t // tm,),
        in_specs=specs + [_const_spec(w.shape) for w in weights],
        out_specs=row,
        out_shape=jax.ShapeDtypeStruct((t, d), F32),
        compiler_params=_params(("parallel",)),
        name=name,
    )(*args, *weights)


DECAY_PARTS = 3


def _head_norm(x, gain, first_half):
    sq = x * x
    ss_a = jnp.sum(jnp.where(first_half, sq, 0.0), axis=-1, keepdims=True)
    ss_b = jnp.sum(jnp.where(first_half, 0.0, sq), axis=-1, keepdims=True)
    inv = lax.rsqrt(jnp.where(first_half, ss_a, ss_b) * (1.0 / HEAD_DIM) + EPS)
    return x * inv * gain


def _store_head_pairs(ref, x, fn=None):
    for p in range(ref.shape[1]):
        tile = x[:, p * LANES:(p + 1) * LANES]
        ref[0, p] = (tile if fn is None else fn(tile)).astype(ref.dtype)


def _decay_rows(parts, head):
    tm = parts[0].shape[1]
    sub = lax.broadcasted_iota(jnp.int32, (SUBLANES, tm), 0)
    rows = jnp.zeros((SUBLANES, tm), F32)
    for n, part in enumerate(parts):
        r = part[head:head + 1, :]
        rows = jnp.where(sub == n, r, jnp.where(sub == n + DECAY_PARTS, -r, rows))
    return rows


def _fox_proj_kernel(h_ref, g_ref, wq_ref, wk_ref, wv_ref, wg_ref, wf_ref, fb_ref, qg_ref, kg_ref,
                     q_ref, k_ref, dec_ref, vt_ref, gate_ref, carry_ref):
    @pl.when(pl.program_id(1) == 0)
    def _():
        carry_ref[...] = jnp.zeros_like(carry_ref)

    first_half = lax.broadcasted_iota(jnp.int32, (1, LANES), 1) < HEAD_DIM
    q_gain = qg_ref[...] * (HEAD_DIM ** -0.5 * LOG2E)
    k_gain = kg_ref[...]
    hn = _rms_norm(h_ref[0], g_ref[...]).astype(BF16)
    _store_head_pairs(q_ref, _dot(hn, wq_ref[...]), lambda t: _head_norm(t, q_gain, first_half))
    _store_head_pairs(k_ref, _dot(hn, wk_ref[...]), lambda t: _head_norm(t, k_gain, first_half))
    vt_ref[0] = _dot(hn, wv_ref[...]).T.astype(BF16)
    _store_head_pairs(gate_ref, jax.nn.sigmoid(_dot(hn, wg_ref[...])))

    log_f = jax.nn.log_sigmoid(_dot(hn, wf_ref[...]) + fb_ref[...])
    tm = log_f.shape[0]
    lower = (lax.broadcasted_iota(jnp.int32, (tm, tm), 0)
             >= lax.broadcasted_iota(jnp.int32, (tm, tm), 1)).astype(BF16)
    hi, mid, lo = _split3(log_f)
    cum = carry_ref[...] + (_dot(lower, hi.astype(BF16)) + _dot(lower, mid.astype(BF16))
                            + _dot(lower, lo.astype(BF16)))
    carry_ref[...] = cum[tm - 1:tm, :]

    parts = _split3(cum.T[:N_HEADS, :] * LOG2E)
    gap = jnp.zeros((HEAD_DIM - SUBLANES, tm), F32)
    for p in range(dec_ref.shape[1]):
        slab = jnp.concatenate([_decay_rows(parts, HEADS_PER_TILE * p + 1), gap,
                                _decay_rows(parts, HEADS_PER_TILE * p), gap], axis=0)
        dec_ref[0, p] = slab.T.astype(BF16)


def _fox_proj(h3d, g, wq, wk, wv, wg, wf_pad, fb_pad, qg_tile, kg_tile):
    b, s, d = h3d.shape
    row = pl.BlockSpec((1, ROW_TILE, d), lambda i, j: (i, j, 0))
    pairs = pl.BlockSpec((1, d // LANES, ROW_TILE, LANES), lambda i, j: (i, 0, j, 0))
    bf = jax.ShapeDtypeStruct((b, d // LANES, s, LANES), BF16)
    consts = (g, wq, wk, wv, wg, wf_pad, fb_pad, qg_tile, kg_tile)
    return pl.pallas_call(
        _fox_proj_kernel,
        grid=(b, s // ROW_TILE),
        in_specs=[row] + [_const_spec(w.shape) for w in consts],
        out_specs=[pairs, pairs, pairs, pl.BlockSpec((1, d, ROW_TILE), lambda i, j: (i, 0, j)), pairs],
        out_shape=[bf, bf, bf, jax.ShapeDtypeStruct((b, d, s), BF16), bf],
        scratch_shapes=[pltpu.VMEM((1, LANES), F32)],
        compiler_params=_params(("parallel", "arbitrary")),
        name="fox_proj",
    )(h3d, *consts)


def _fox_attn_kernel(q_ref, k_ref, dec_ref, vt_ref, o_ref):
    s_len = q_ref.shape[2]
    lane = lax.broadcasted_iota(jnp.int32, (Q_TILE, LANES), 1).astype(F32).astype(BF16)
    one = jnp.ones((Q_TILE, LANES), BF16)
    visible = (lax.broadcasted_iota(jnp.int32, (Q_TILE, Q_TILE), 0)
               <= lax.broadcasted_iota(jnp.int32, (Q_TILE, Q_TILE), 1))
    nt = (((1,), (1,)), ((), ()))

    def with_decay(x, dec, j, ones_first):
        base = HEAD_DIM * (1 - j)
        start = base if ones_first else base + DECAY_PARTS
        aug = jnp.where(lane >= start, jnp.where(lane < start + DECAY_PARTS, one, dec), dec)
        own = (lane < HEAD_DIM) if j == 0 else (lane >= HEAD_DIM)
        return jnp.where(own, x, aug)

    n_tiles = s_len // Q_TILE
    rows = lambda ref, i: ref[0, 0, i * Q_TILE:(i + 1) * Q_TILE, :]
    k_heads = [jnp.concatenate([with_decay(rows(k_ref, i), rows(dec_ref, i), j, ones_first=True)
                                for i in range(n_tiles)], axis=0) for j in range(HEADS_PER_TILE)]

    def scores(i, j):
        qh = with_decay(rows(q_ref, i), rows(dec_ref, i), j, ones_first=False)
        return lax.dot_general(k_heads[j][:(i + 1) * Q_TILE], qh, nt, preferred_element_type=F32)

    def attend(i, j, st):
        q0, q1 = i * Q_TILE, (i + 1) * Q_TILE
        diag = jnp.where(visible, st[q0:], -jnp.inf)
        m = jnp.max(diag, axis=0, keepdims=True)
        if i:
            past = st[:q0]
            m = jnp.maximum(m, jnp.max(past, axis=0, keepdims=True))
            pt = jnp.concatenate([jnp.exp2(past - m).astype(BF16),
                                  jnp.exp2(diag - m).astype(BF16)], axis=0)
        else:
            pt = jnp.exp2(diag - m).astype(BF16)
        vt = jnp.concatenate([vt_ref[0, j * HEAD_DIM:(j + 1) * HEAD_DIM, :q1],
                              jnp.ones((BF16_SUBLANES, q1), BF16)], axis=0)
        acc = _dot(vt, pt)
        return acc[:HEAD_DIM] / acc[HEAD_DIM:HEAD_DIM + 1]

    units = [(i, j) for i in range(n_tiles) for j in range(HEADS_PER_TILE)]
    pending = [scores(*u) for u in units[:SCORE_LOOKAHEAD]]
    outs = []
    for n, (i, j) in enumerate(units):
        if n + SCORE_LOOKAHEAD < len(units):
            pending.append(scores(*units[n + SCORE_LOOKAHEAD]))
        outs.append(attend(i, j, pending.pop(0)))
        if len(outs) == HEADS_PER_TILE:
            o_ref[0, 0, i * Q_TILE:(i + 1) * Q_TILE, :] = jnp.concatenate(outs, axis=0).T.astype(BF16)
            outs = []


def _fox_attn(q, k, dec, vt):
    b, n_pairs, s, _ = q.shape
    tile = pl.BlockSpec((1, 1, s, LANES), lambda i, j: (i, j, 0, 0))
    return pl.pallas_call(
        _fox_attn_kernel,
        grid=(b, n_pairs),
        in_specs=[tile, tile, tile, pl.BlockSpec((1, LANES, s), lambda i, j: (i, j, 0))],
        out_specs=tile,
        out_shape=jax.ShapeDtypeStruct(q.shape, BF16),
        compiler_params=_params(("parallel", "parallel")),
        name="fox_attention",
    )(q, k, dec, vt)


def kernel(x, norm_mix_g, norm_ffn_g, a_w_in, a_ln_g, a_ln_b, a_w_s, a_b_s, a_w_out,
           b_w_in, b_f_bias, b_q_norm_g, b_k_norm_g, b_w_out,
           ffn_w_gate, ffn_w_up, ffn_w_down):
    bsz, seq, d = x.shape
    row = lambda p: p.reshape(1, -1).astype(F32)
    bf = lambda w: w.astype(BF16)

    b_s_tile = jnp.repeat(a_b_s[0].T, LANES, axis=1)
    h = _gmlp_layer(x.reshape(bsz * seq, d), row(norm_mix_g[0]), bf(a_w_in[0]),
                    row(a_ln_g[0]), row(a_ln_b[0]), a_w_s[0], b_s_tile, bf(a_w_out[0]))
    h = _ffn_layer(h, row(norm_ffn_g[0]), bf(ffn_w_gate[0]), bf(ffn_w_up[0]), bf(ffn_w_down[0]))

    w_in = b_w_in[0]
    wq, wk, wv, wg = (bf(w_in[:, i * d:(i + 1) * d]) for i in range(4))
    wf_pad = bf(jnp.pad(w_in[:, 4 * d:], ((0, 0), (0, LANES - N_HEADS))))
    fb_pad = jnp.pad(b_f_bias[0], (0, LANES - N_HEADS)).reshape(1, LANES)
    qg_tile = jnp.tile(b_q_norm_g[0], HEADS_PER_TILE).reshape(1, LANES)
    kg_tile = jnp.tile(b_k_norm_g[0], HEADS_PER_TILE).reshape(1, LANES)
    q, k, dec, vt, gate = _fox_proj(h.reshape(bsz, seq, d), row(norm_mix_g[1]),
                                    wq, wk, wv, wg, wf_pad, fb_pad, qg_tile, kg_tile)
    attn = _fox_attn(q, k, dec, vt)
    h = _ffn_layer(h, row(norm_ffn_g[1]), bf(ffn_w_gate[1]), bf(ffn_w_up[1]), bf(ffn_w_down[1]),
                   attn=attn, gate=gate, w_o=bf(b_w_out[0]))
    return h.reshape(bsz, seq, d)
```

```python
import math

import jax
import jax.numpy as jnp
from jax import lax
from jax.experimental import pallas as pl
from jax.experimental.pallas import tpu as pltpu

F32 = jnp.float32
BF16 = jnp.bfloat16

EPS = 1e-6
CHUNK = 64
GMLP_BLOCK = 128
GMLP_GROUPS = 8
N_HEADS = 16
HEAD_DIM = 64
LANES = 128
SUBLANES = 8
BF16_SUBLANES = 16
HEADS_PER_TILE = LANES // HEAD_DIM
LOG2E = math.log2(math.e)

ROW_TILE = 512
GMLP_ROW_TILE = 1024
GMLP_SUB_TILE = 256
GMLP_LOOKAHEAD = 1
FFN_ROW_TILE = 512
Q_TILE = 256
SCORE_LOOKAHEAD = 4
VMEM_LIMIT_BYTES = 56 * 1024 * 1024


def _const_spec(shape):
    nd = len(shape)
    return pl.BlockSpec(shape, lambda *_: (0,) * nd, pipeline_mode=pl.Buffered(1))


def _params(semantics):
    return pltpu.CompilerParams(dimension_semantics=semantics,
                                vmem_limit_bytes=VMEM_LIMIT_BYTES)


def _rms_norm(x, g):
    ms = jnp.mean(x * x, axis=-1, keepdims=True)
    return x * lax.rsqrt(ms + EPS) * g


def _dot(a, b):
    return jnp.dot(a, b, preferred_element_type=F32)


def _gelu_exact(x):
    return 0.5 * x * (1.0 + lax.erf(x * math.sqrt(0.5)))


def _split3(x):
    hi = x.astype(BF16).astype(F32)
    r = x - hi
    mid = r.astype(BF16).astype(F32)
    lo = (r - mid).astype(BF16).astype(F32)
    return hi, mid, lo


def _gmlp_kernel(x_ref, g_ref, win_ref, lng_ref, lnb_ref, ws_ref, bs_ref, wout_ref, o_ref):
    d = x_ref.shape[-1]
    t_chunk = lax.broadcasted_iota(jnp.int32, (GMLP_BLOCK, GMLP_BLOCK), 0) // CHUNK
    s_chunk = lax.broadcasted_iota(jnp.int32, (GMLP_BLOCK, GMLP_BLOCK), 1) // CHUNK
    visible = s_chunk <= t_chunk
    ws = [jnp.where(visible, ws_ref[g], 0.0).astype(BF16) for g in range(GMLP_GROUPS)]
    bias = bs_ref[...]

    def expand(r):
        x = x_ref[r * GMLP_SUB_TILE:(r + 1) * GMLP_SUB_TILE, :]
        return _dot(_rms_norm(x, g_ref[...]).astype(BF16), win_ref[...])

    def mix(r, z):
        z = _gelu_exact(z)
        u = z[:, :d]
        v = z[:, d:]
        mu = jnp.mean(v, axis=-1, keepdims=True)
        vc = v - mu
        var = jnp.mean(vc * vc, axis=-1, keepdims=True)
        vn = (vc * lax.rsqrt(var + EPS) * lng_ref[...] + lnb_ref[...]).astype(BF16)
        blocks = []
        for n in range(GMLP_SUB_TILE // GMLP_BLOCK):
            vb = vn[n * GMLP_BLOCK:(n + 1) * GMLP_BLOCK]
            cols = [_dot(ws[g], vb[:, g * LANES:(g + 1) * LANES]) for g in range(GMLP_GROUPS)]
            blocks.append(jnp.concatenate(cols, axis=1) + bias)
        gated = (u * jnp.concatenate(blocks, axis=0)).astype(BF16)
        rows = slice(r * GMLP_SUB_TILE, (r + 1) * GMLP_SUB_TILE)
        o_ref[rows, :] = x_ref[rows, :] + _dot(gated, wout_ref[...])

    n_sub = x_ref.shape[0] // GMLP_SUB_TILE
    pending = [expand(r) for r in range(min(GMLP_LOOKAHEAD, n_sub))]
    for r in range(n_sub):
        if r + GMLP_LOOKAHEAD < n_sub:
            pending.append(expand(r + GMLP_LOOKAHEAD))
        mix(r, pending.pop(0))


def _gmlp_layer(x2d, g, w_in, ln_g, ln_b, w_s, b_s_tile, w_out):
    t, d = x2d.shape
    row = pl.BlockSpec((GMLP_ROW_TILE, d), lambda i: (i, 0))
    return pl.pallas_call(
        _gmlp_kernel,
        grid=(t // GMLP_ROW_TILE,),
        in_specs=[row, _const_spec(g.shape), _const_spec(w_in.shape), _const_spec(ln_g.shape),
                  _const_spec(ln_b.shape), _const_spec(w_s.shape), _const_spec(b_s_tile.shape),
                  _const_spec(w_out.shape)],
        out_specs=row,
        out_shape=jax.ShapeDtypeStruct((t, d), F32),
        compiler_params=_params(("parallel",)),
        name="gmlp_mixer",
    )(x2d, g, w_in, ln_g, ln_b, w_s, b_s_tile, w_out)


def _swiglu(h, g, wg_ref, wu_ref, wd_ref):
    hn = _rms_norm(h, g).astype(BF16)
    a = _dot(hn, wg_ref[...])
    b = _dot(hn, wu_ref[...])
    act = (a * jax.nn.sigmoid(a) * b).astype(BF16)
    return h + _dot(act, wd_ref[...])


def _ffn_kernel(h_ref, g_ref, wg_ref, wu_ref, wd_ref, o_ref):
    o_ref[...] = _swiglu(h_ref[...], g_ref[...], wg_ref, wu_ref, wd_ref)


def _load_head_pairs(ref):
    return jnp.concatenate([ref[0, p] for p in range(ref.shape[1])], axis=1).astype(F32)


def _proj_ffn_kernel(h_ref, a_ref, gate_ref, wo_ref, g_ref, wg_ref, wu_ref, wd_ref, o_ref):
    a = (_load_head_pairs(a_ref) * _load_head_pairs(gate_ref)).astype(BF16)
    h = h_ref[...] + _dot(a, wo_ref[...])
    o_ref[...] = _swiglu(h, g_ref[...], wg_ref, wu_ref, wd_ref)


def _ffn_layer(h2d, g, w_gate, w_up, w_down, attn=None, gate=None, w_o=None):
    t, d = h2d.shape
    tm = FFN_ROW_TILE
    row = pl.BlockSpec((tm, d), lambda i: (i, 0))
    weights = [g, w_gate, w_up, w_down]
    if attn is None:
        kernel, args, specs, name = _ffn_kernel, [h2d], [row], "swiglu_ffn"
    else:
        tiles_per_seq = attn.shape[2] // tm
        pairs = pl.BlockSpec((1, attn.shape[1], tm, LANES),
                             lambda i: (i // tiles_per_seq, 0, i % tiles_per_seq, 0))
        kernel, args, specs, name = _proj_ffn_kernel, [h2d, attn, gate, w_o], \
            [row, pairs, pairs, _const_spec(w_o.shape)], "attn_out_swiglu_ffn"
    return pl.pallas_call(
        kernel,
        grid=(t // tm,),
        in_specs=specs + [_const_spec(w.shape) for w in weights],
        out_specs=row,
        out_shape=jax.ShapeDtypeStruct((t, d), F32),
        compiler_params=_params(("parallel",)),
        name=name,
    )(*args, *weights)


DECAY_PARTS = 3


def _head_norm(x, gain, first_half):
    sq = x * x
    ss_a = jnp.sum(jnp.where(first_half, sq, 0.0), axis=-1, keepdims=True)
    ss_b = jnp.sum(jnp.where(first_half, 0.0, sq), axis=-1, keepdims=True)
    inv = lax.rsqrt(jnp.where(first_half, ss_a, ss_b) * (1.0 / HEAD_DIM) + EPS)
    return x * inv * gain


def _store_head_pairs(ref, x, fn=None):
    for p in range(ref.shape[1]):
        tile = x[:, p * LANES:(p + 1) * LANES]
        ref[0, p] = (tile if fn is None else fn(tile)).astype(ref.dtype)


def _decay_rows(parts, head):
    tm = parts[0].shape[1]
    sub = lax.broadcasted_iota(jnp.int32, (SUBLANES, tm), 0)
    rows = jnp.zeros((SUBLANES, tm), F32)
    for n, part in enumerate(parts):
        r = part[head:head + 1, :]
        rows = jnp.where(sub == n, r, jnp.where(sub == n + DECAY_PARTS, -r, rows))
    return rows


def _fox_proj_kernel(h_ref, g_ref, wq_ref, wk_ref, wv_ref, wg_ref, wf_ref, fb_ref, qg_ref, kg_ref,
                     q_ref, k_ref, dec_ref, vt_ref, gate_ref, carry_ref):
    @pl.when(pl.program_id(1) == 0)
    def _():
        carry_ref[...] = jnp.zeros_like(carry_ref)

    first_half = lax.broadcasted_iota(jnp.int32, (1, LANES), 1) < HEAD_DIM
    q_gain = qg_ref[...] * (HEAD_DIM ** -0.5 * LOG2E)
    k_gain = kg_ref[...]
    hn = _rms_norm(h_ref[0], g_ref[...]).astype(BF16)

    log_f = jax.nn.log_sigmoid(_dot(hn, wf_ref[...]) + fb_ref[...])
    q = _dot(hn, wq_ref[...])
    tm = log_f.shape[0]
    lower = (lax.broadcasted_iota(jnp.int32, (tm, tm), 0)
             >= lax.broadcasted_iota(jnp.int32, (tm, tm), 1)).astype(BF16)
    hi, mid, lo = _split3(log_f)
    cum = carry_ref[...] + (_dot(lower, hi.astype(BF16)) + _dot(lower, mid.astype(BF16))
                            + _dot(lower, lo.astype(BF16)))
    carry_ref[...] = cum[tm - 1:tm, :]
    k = _dot(hn, wk_ref[...])
    v = _dot(hn, wv_ref[...])
    g = _dot(hn, wg_ref[...])

    _store_head_pairs(q_ref, q, lambda t: _head_norm(t, q_gain, first_half))
    parts = _split3(cum.T[:N_HEADS, :] * LOG2E)
    gap = jnp.zeros((HEAD_DIM - SUBLANES, tm), F32)
    for p in range(dec_ref.shape[1]):
        slab = jnp.concatenate([_decay_rows(parts, HEADS_PER_TILE * p + 1), gap,
                                _decay_rows(parts, HEADS_PER_TILE * p), gap], axis=0)
        dec_ref[0, p] = slab.T.astype(BF16)
    _store_head_pairs(k_ref, k, lambda t: _head_norm(t, k_gain, first_half))
    vt_ref[0] = v.T.astype(BF16)
    _store_head_pairs(gate_ref, jax.nn.sigmoid(g))


def _fox_proj(h3d, g, wq, wk, wv, wg, wf_pad, fb_pad, qg_tile, kg_tile):
    b, s, d = h3d.shape
    row = pl.BlockSpec((1, ROW_TILE, d), lambda i, j: (i, j, 0))
    pairs = pl.BlockSpec((1, d // LANES, ROW_TILE, LANES), lambda i, j: (i, 0, j, 0))
    bf = jax.ShapeDtypeStruct((b, d // LANES, s, LANES), BF16)
    consts = (g, wq, wk, wv, wg, wf_pad, fb_pad, qg_tile, kg_tile)
    return pl.pallas_call(
        _fox_proj_kernel,
        grid=(b, s // ROW_TILE),
        in_specs=[row] + [_const_spec(w.shape) for w in consts],
        out_specs=[pairs, pairs, pairs, pl.BlockSpec((1, d, ROW_TILE), lambda i, j: (i, 0, j)), pairs],
        out_shape=[bf, bf, bf, jax.ShapeDtypeStruct((b, d, s), BF16), bf],
        scratch_shapes=[pltpu.VMEM((1, LANES), F32)],
        compiler_params=_params(("parallel", "arbitrary")),
        name="fox_proj",
    )(h3d, *consts)


def _fox_attn_kernel(q_ref, k_ref, dec_ref, vt_ref, o_ref):
    s_len = q_ref.shape[2]
    lane = lax.broadcasted_iota(jnp.int32, (Q_TILE, LANES), 1).astype(F32).astype(BF16)
    one = jnp.ones((Q_TILE, LANES), BF16)
    visible = (lax.broadcasted_iota(jnp.int32, (Q_TILE, Q_TILE), 0)
               <= lax.broadcasted_iota(jnp.int32, (Q_TILE, Q_TILE), 1))
    nt = (((1,), (1,)), ((), ()))

    def with_decay(x, dec, j, ones_first):
        base = HEAD_DIM * (1 - j)
        start = base if ones_first else base + DECAY_PARTS
        aug = jnp.where(lane >= start, jnp.where(lane < start + DECAY_PARTS, one, dec), dec)
        own = (lane < HEAD_DIM) if j == 0 else (lane >= HEAD_DIM)
        return jnp.where(own, x, aug)

    n_tiles = s_len // Q_TILE
    rows = lambda ref, i: ref[0, 0, i * Q_TILE:(i + 1) * Q_TILE, :]
    k_heads = [jnp.concatenate([with_decay(rows(k_ref, i), rows(dec_ref, i), j, ones_first=True)
                                for i in range(n_tiles)], axis=0) for j in range(HEADS_PER_TILE)]

    def scores(i, j):
        qh = with_decay(rows(q_ref, i), rows(dec_ref, i), j, ones_first=False)
        return lax.dot_general(k_heads[j][:(i + 1) * Q_TILE], qh, nt, preferred_element_type=F32)

    def attend(i, j, st):
        q0, q1 = i * Q_TILE, (i + 1) * Q_TILE
        diag = jnp.where(visible, st[q0:], -jnp.inf)
        m = jnp.max(diag, axis=0, keepdims=True)
        if i:
            past = st[:q0]
            m = jnp.maximum(m, jnp.max(past, axis=0, keepdims=True))
            pt = jnp.concatenate([jnp.exp2(past - m).astype(BF16),
                                  jnp.exp2(diag - m).astype(BF16)], axis=0)
        else:
            pt = jnp.exp2(diag - m).astype(BF16)
        vt = jnp.concatenate([vt_ref[0, j * HEAD_DIM:(j + 1) * HEAD_DIM, :q1],
                              jnp.ones((BF16_SUBLANES, q1), BF16)], axis=0)
        acc = _dot(vt, pt)
        return acc[:HEAD_DIM] / acc[HEAD_DIM:HEAD_DIM + 1]

    units = [(i, j) for i in range(n_tiles) for j in range(HEADS_PER_TILE)]
    pending = [scores(*u) for u in units[:SCORE_LOOKAHEAD]]
    outs = []
    for n, (i, j) in enumerate(units):
        if n + SCORE_LOOKAHEAD < len(units):
            pending.append(scores(*units[n + SCORE_LOOKAHEAD]))
        outs.append(attend(i, j, pending.pop(0)))
        if len(outs) == HEADS_PER_TILE:
            o_ref[0, 0, i * Q_TILE:(i + 1) * Q_TILE, :] = jnp.concatenate(outs, axis=0).T.astype(BF16)
            outs = []


def _fox_attn(q, k, dec, vt):
    b, n_pairs, s, _ = q.shape
    tile = pl.BlockSpec((1, 1, s, LANES), lambda i, j: (i, j, 0, 0))
    return pl.pallas_call(
        _fox_attn_kernel,
        grid=(b, n_pairs),
        in_specs=[tile, tile, tile, pl.BlockSpec((1, LANES, s), lambda i, j: (i, j, 0))],
        out_specs=tile,
        out_shape=jax.ShapeDtypeStruct(q.shape, BF16),
        compiler_params=_params(("parallel", "parallel")),
        name="fox_attention",
    )(q, k, dec, vt)


def kernel(x, norm_mix_g, norm_ffn_g, a_w_in, a_ln_g, a_ln_b, a_w_s, a_b_s, a_w_out,
           b_w_in, b_f_bias, b_q_norm_g, b_k_norm_g, b_w_out,
           ffn_w_gate, ffn_w_up, ffn_w_down):
    bsz, seq, d = x.shape
    row = lambda p: p.reshape(1, -1).astype(F32)
    bf = lambda w: w.astype(BF16)

    b_s_tile = jnp.repeat(a_b_s[0].T, LANES, axis=1)
    h = _gmlp_layer(x.reshape(bsz * seq, d), row(norm_mix_g[0]), bf(a_w_in[0]),
                    row(a_ln_g[0]), row(a_ln_b[0]), a_w_s[0], b_s_tile, bf(a_w_out[0]))
    h = _ffn_layer(h, row(norm_ffn_g[0]), bf(ffn_w_gate[0]), bf(ffn_w_up[0]), bf(ffn_w_down[0]))

    w_in = b_w_in[0]
    wq, wk, wv, wg = (bf(w_in[:, i * d:(i + 1) * d]) for i in range(4))
    wf_pad = bf(jnp.pad(w_in[:, 4 * d:], ((0, 0), (0, LANES - N_HEADS))))
    fb_pad = jnp.pad(b_f_bias[0], (0, LANES - N_HEADS)).reshape(1, LANES)
    qg_tile = jnp.tile(b_q_norm_g[0], HEADS_PER_TILE).reshape(1, LANES)
    kg_tile = jnp.tile(b_k_norm_g[0], HEADS_PER_TILE).reshape(1, LANES)
    q, k, dec, vt, gate = _fox_proj(h.reshape(bsz, seq, d), row(norm_mix_g[1]),
                                    wq, wk, wv, wg, wf_pad, fb_pad, qg_tile, kg_tile)
    attn = _fox_attn(q, k, dec, vt)
    h = _ffn_layer(h, row(norm_ffn_g[1]), bf(ffn_w_gate[1]), bf(ffn_w_up[1]), bf(ffn_w_down[1]),
                   attn=attn, gate=gate, w_o=bf(b_w_out[0]))
    return h.reshape(bsz, seq, d)
```

```python
import math

import jax
import jax.numpy as jnp
from jax import lax
from jax.experimental import pallas as pl
from jax.experimental.pallas import tpu as pltpu

F32 = jnp.float32
BF16 = jnp.bfloat16

EPS = 1e-6
CHUNK = 64
GMLP_BLOCK = 128
GMLP_GROUPS = 8
N_HEADS = 16
HEAD_DIM = 64
LANES = 128
SUBLANES = 8
BF16_SUBLANES = 16
HEADS_PER_TILE = LANES // HEAD_DIM
V_ROWS = HEAD_DIM + BF16_SUBLANES
LOG2E = math.log2(math.e)

ROW_TILE = 512
GMLP_ROW_TILE = 1024
GMLP_SUB_TILE = 256
GMLP_LOOKAHEAD = 1
FFN_ROW_TILE = 512
Q_TILE = 256
SCORE_LOOKAHEAD = 4
VMEM_LIMIT_BYTES = 56 * 1024 * 1024


def _const_spec(shape):
    nd = len(shape)
    return pl.BlockSpec(shape, lambda *_: (0,) * nd, pipeline_mode=pl.Buffered(1))


def _params(semantics):
    return pltpu.CompilerParams(dimension_semantics=semantics,
                                vmem_limit_bytes=VMEM_LIMIT_BYTES)


def _rms_norm(x, g):
    ms = jnp.mean(x * x, axis=-1, keepdims=True)
    return x * lax.rsqrt(ms + EPS) * g


def _dot(a, b):
    return jnp.dot(a, b, preferred_element_type=F32)


def _gelu_exact(x):
    return 0.5 * x * (1.0 + lax.erf(x * math.sqrt(0.5)))


def _split3(x):
    hi = x.astype(BF16).astype(F32)
    r = x - hi
    mid = r.astype(BF16).astype(F32)
    lo = (r - mid).astype(BF16).astype(F32)
    return hi, mid, lo


def _gmlp_kernel(x_ref, g_ref, win_ref, lng_ref, lnb_ref, ws_ref, bs_ref, wout_ref, o_ref):
    d = x_ref.shape[-1]
    t_chunk = lax.broadcasted_iota(jnp.int32, (GMLP_BLOCK, GMLP_BLOCK), 0) // CHUNK
    s_chunk = lax.broadcasted_iota(jnp.int32, (GMLP_BLOCK, GMLP_BLOCK), 1) // CHUNK
    visible = s_chunk <= t_chunk
    ws = [jnp.where(visible, ws_ref[g], 0.0).astype(BF16) for g in range(GMLP_GROUPS)]
    bias = bs_ref[...]

    def expand(r):
        x = x_ref[r * GMLP_SUB_TILE:(r + 1) * GMLP_SUB_TILE, :]
        return _dot(_rms_norm(x, g_ref[...]).astype(BF16), win_ref[...])

    def mix(r, z):
        z = _gelu_exact(z)
        u = z[:, :d]
        v = z[:, d:]
        mu = jnp.mean(v, axis=-1, keepdims=True)
        vc = v - mu
        var = jnp.mean(vc * vc, axis=-1, keepdims=True)
        vn = (vc * lax.rsqrt(var + EPS) * lng_ref[...] + lnb_ref[...]).astype(BF16)
        blocks = []
        for n in range(GMLP_SUB_TILE // GMLP_BLOCK):
            vb = vn[n * GMLP_BLOCK:(n + 1) * GMLP_BLOCK]
            cols = [_dot(ws[g], vb[:, g * LANES:(g + 1) * LANES]) for g in range(GMLP_GROUPS)]
            blocks.append(jnp.concatenate(cols, axis=1) + bias)
        gated = (u * jnp.concatenate(blocks, axis=0)).astype(BF16)
        rows = slice(r * GMLP_SUB_TILE, (r + 1) * GMLP_SUB_TILE)
        o_ref[rows, :] = x_ref[rows, :] + _dot(gated, wout_ref[...])

    n_sub = x_ref.shape[0] // GMLP_SUB_TILE
    pending = [expand(r) for r in range(min(GMLP_LOOKAHEAD, n_sub))]
    for r in range(n_sub):
        if r + GMLP_LOOKAHEAD < n_sub:
            pending.append(expand(r + GMLP_LOOKAHEAD))
        mix(r, pending.pop(0))


def _gmlp_layer(x2d, g, w_in, ln_g, ln_b, w_s, b_s_tile, w_out):
    t, d = x2d.shape
    row = pl.BlockSpec((GMLP_ROW_TILE, d), lambda i: (i, 0))
    return pl.pallas_call(
        _gmlp_kernel,
        grid=(t // GMLP_ROW_TILE,),
        in_specs=[row, _const_spec(g.shape), _const_spec(w_in.shape), _const_spec(ln_g.shape),
                  _const_spec(ln_b.shape), _const_spec(w_s.shape), _const_spec(b_s_tile.shape),
                  _const_spec(w_out.shape)],
        out_specs=row,
        out_shape=jax.ShapeDtypeStruct((t, d), F32),
        compiler_params=_params(("parallel",)),
        name="gmlp_mixer",
    )(x2d, g, w_in, ln_g, ln_b, w_s, b_s_tile, w_out)


def _swiglu(h, g, wg_ref, wu_ref, wd_ref):
    hn = _rms_norm(h, g).astype(BF16)
    a = _dot(hn, wg_ref[...])
    b = _dot(hn, wu_ref[...])
    act = (a * jax.nn.sigmoid(a) * b).astype(BF16)
    return h + _dot(act, wd_ref[...])


def _ffn_kernel(h_ref, g_ref, wg_ref, wu_ref, wd_ref, o_ref):
    o_ref[...] = _swiglu(h_ref[...], g_ref[...], wg_ref, wu_ref, wd_ref)


def _load_head_pairs(ref):
    return jnp.concatenate([ref[0, p] for p in range(ref.shape[1])], axis=1).astype(F32)


def _proj_ffn_kernel(h_ref, a_ref, gate_ref, wo_ref, g_ref, wg_ref, wu_ref, wd_ref, o_ref):
    a = (_load_head_pairs(a_ref) * _load_head_pairs(gate_ref)).astype(BF16)
    h = h_ref[...] + _dot(a, wo_ref[...])
    o_ref[...] = _swiglu(h, g_ref[...], wg_ref, wu_ref, wd_ref)


def _layer_spec(stacked, layer):
    nd = stacked.ndim - 1
    return pl.BlockSpec((None,) + stacked.shape[1:], lambda *_: (layer,) + (0,) * nd,
                        pipeline_mode=pl.Buffered(1))


def _ffn_layer(h2d, layer, g, w_gate, w_up, w_down, attn=None, gate=None, w_o=None):
    t, d = h2d.shape
    tm = FFN_ROW_TILE
    row = pl.BlockSpec((tm, d), lambda i: (i, 0))
    weights = [g, w_gate, w_up, w_down]
    weight_specs = [_const_spec(g.shape)] + [_layer_spec(w, layer) for w in weights[1:]]
    if attn is None:
        kernel, args, specs, name = _ffn_kernel, [h2d], [row], "swiglu_ffn"
    else:
        tiles_per_seq = attn.shape[2] // tm
        pairs = pl.BlockSpec((1, attn.shape[1], tm, LANES),
                             lambda i: (i // tiles_per_seq, 0, i % tiles_per_seq, 0))
        kernel, args, specs, name = _proj_ffn_kernel, [h2d, attn, gate, w_o], \
            [row, pairs, pairs, _const_spec(w_o.shape)], "attn_out_swiglu_ffn"
    return pl.pallas_call(
        kernel,
        grid=(t // tm,),
        in_specs=specs + weight_specs,
        out_specs=row,
        out_shape=jax.ShapeDtypeStruct((t, d), F32),
        compiler_params=_params(("parallel",)),
        name=name,
    )(*args, *weights)


DECAY_PARTS = 3


def _head_norm(x, gain, first_half):
    sq = x * x
    ss_a = jnp.sum(jnp.where(first_half, sq, 0.0), axis=-1, keepdims=True)
    ss_b = jnp.sum(jnp.where(first_half, 0.0, sq), axis=-1, keepdims=True)
    inv = lax.rsqrt(jnp.where(first_half, ss_a, ss_b) * (1.0 / HEAD_DIM) + EPS)
    return x * inv * gain


def _store_head_pairs(ref, x, fn=None):
    for p in range(ref.shape[1]):
        tile = x[:, p * LANES:(p + 1) * LANES]
        ref[0, p] = (tile if fn is None else fn(tile)).astype(ref.dtype)


def _decay_rows(parts, head):
    tm = parts[0].shape[1]
    sub = lax.broadcasted_iota(jnp.int32, (SUBLANES, tm), 0)
    rows = jnp.zeros((SUBLANES, tm), F32)
    for n, part in enumerate(parts):
        r = part[head:head + 1, :]
        rows = jnp.where(sub == n, r, jnp.where(sub == n + DECAY_PARTS, -r, rows))
    return rows


def _fox_proj_kernel(h_ref, g_ref, win_ref, wf_ref, fb_ref, qg_ref, kg_ref,
                     q_ref, k_ref, dec_ref, vt_ref, gate_ref, carry_ref):
    @pl.when(pl.program_id(1) == 0)
    def _():
        carry_ref[...] = jnp.zeros_like(carry_ref)

    d = h_ref.shape[-1]
    wq_ref, wk_ref, wv_ref, wg_ref = (win_ref.at[:, n * d:(n + 1) * d] for n in range(4))
    first_half = lax.broadcasted_iota(jnp.int32, (1, LANES), 1) < HEAD_DIM
    q_gain = qg_ref[...] * (HEAD_DIM ** -0.5 * LOG2E)
    k_gain = kg_ref[...]
    hn = _rms_norm(h_ref[0], g_ref[...]).astype(BF16)

    log_f = jax.nn.log_sigmoid(_dot(hn, wf_ref[...]) + fb_ref[...])
    q = _dot(hn, wq_ref[...])
    tm = log_f.shape[0]
    lower = (lax.broadcasted_iota(jnp.int32, (tm, tm), 0)
             >= lax.broadcasted_iota(jnp.int32, (tm, tm), 1)).astype(BF16)
    hi, mid, lo = _split3(log_f)
    cum = carry_ref[...] + (_dot(lower, hi.astype(BF16)) + _dot(lower, mid.astype(BF16))
                            + _dot(lower, lo.astype(BF16)))
    carry_ref[...] = cum[tm - 1:tm, :]
    k = _dot(hn, wk_ref[...])
    v = _dot(hn, wv_ref[...])
    g = _dot(hn, wg_ref[...])

    _store_head_pairs(q_ref, q, lambda t: _head_norm(t, q_gain, first_half))
    parts = _split3(cum.T[:N_HEADS, :] * LOG2E)
    gap = jnp.zeros((HEAD_DIM - SUBLANES, tm), F32)
    for p in range(dec_ref.shape[1]):
        slab = jnp.concatenate([_decay_rows(parts, HEADS_PER_TILE * p + 1), gap,
                                _decay_rows(parts, HEADS_PER_TILE * p), gap], axis=0)
        dec_ref[0, p] = slab.T.astype(BF16)
    _store_head_pairs(k_ref, k, lambda t: _head_norm(t, k_gain, first_half))
    vt = v.T
    ones_rows = jnp.ones((BF16_SUBLANES, tm), F32)
    vt_ref[0] = jnp.concatenate(
        [piece for h in range(N_HEADS) for piece in (vt[h * HEAD_DIM:(h + 1) * HEAD_DIM], ones_rows)],
        axis=0).astype(BF16)
    _store_head_pairs(gate_ref, jax.nn.sigmoid(g))


def _fox_proj(h3d, g, w_in, wf_pad, fb_pad, qg_tile, kg_tile):
    b, s, d = h3d.shape
    row = pl.BlockSpec((1, ROW_TILE, d), lambda i, j: (i, j, 0))
    pairs = pl.BlockSpec((1, d // LANES, ROW_TILE, LANES), lambda i, j: (i, 0, j, 0))
    bf = jax.ShapeDtypeStruct((b, d // LANES, s, LANES), BF16)
    vt_rows = N_HEADS * V_ROWS
    consts = (g, w_in, wf_pad, fb_pad, qg_tile, kg_tile)
    return pl.pallas_call(
        _fox_proj_kernel,
        grid=(b, s // ROW_TILE),
        in_specs=[row] + [_const_spec(w.shape) for w in consts],
        out_specs=[pairs, pairs, pairs, pl.BlockSpec((1, vt_rows, ROW_TILE), lambda i, j: (i, 0, j)), pairs],
        out_shape=[bf, bf, bf, jax.ShapeDtypeStruct((b, vt_rows, s), BF16), bf],
        scratch_shapes=[pltpu.VMEM((1, LANES), F32)],
        compiler_params=_params(("parallel", "arbitrary")),
        name="fox_proj",
    )(h3d, *consts)


def _fox_attn_kernel(q_ref, k_ref, dec_ref, vt_ref, o_ref):
    s_len = q_ref.shape[2]
    lane = lax.broadcasted_iota(jnp.int32, (Q_TILE, LANES), 1).astype(F32).astype(BF16)
    one = jnp.ones((Q_TILE, LANES), BF16)
    visible = (lax.broadcasted_iota(jnp.int32, (Q_TILE, Q_TILE), 0)
               <= lax.broadcasted_iota(jnp.int32, (Q_TILE, Q_TILE), 1))
    nt = (((1,), (1,)), ((), ()))

    def with_decay(x, dec, j, ones_first):
        base = HEAD_DIM * (1 - j)
        start = base if ones_first else base + DECAY_PARTS
        aug = jnp.where(lane >= start, jnp.where(lane < start + DECAY_PARTS, one, dec), dec)
        own = (lane < HEAD_DIM) if j == 0 else (lane >= HEAD_DIM)
        return jnp.where(own, x, aug)

    n_tiles = s_len // Q_TILE
    rows = lambda ref, i: ref[0, 0, i * Q_TILE:(i + 1) * Q_TILE, :]
    k_heads = [jnp.concatenate([with_decay(rows(k_ref, i), rows(dec_ref, i), j, ones_first=True)
                                for i in range(n_tiles)], axis=0) for j in range(HEADS_PER_TILE)]

    def scores(i, j):
        qh = with_decay(rows(q_ref, i), rows(dec_ref, i), j, ones_first=False)
        return lax.dot_general(k_heads[j][:(i + 1) * Q_TILE], qh, nt, preferred_element_type=F32)

    def attend(i, j, st):
        q0, q1 = i * Q_TILE, (i + 1) * Q_TILE
        diag = jnp.where(visible, st[q0:], -jnp.inf)
        m = jnp.max(diag, axis=0, keepdims=True)
        if i:
            past = st[:q0]
            m = jnp.maximum(m, jnp.max(past, axis=0, keepdims=True))
            pt = jnp.concatenate([jnp.exp2(past - m).astype(BF16),
                                  jnp.exp2(diag - m).astype(BF16)], axis=0)
        else:
            pt = jnp.exp2(diag - m).astype(BF16)
        acc = _dot(vt_ref[0, j * V_ROWS:(j + 1) * V_ROWS, :q1], pt)
        return acc[:HEAD_DIM] / acc[HEAD_DIM:HEAD_DIM + 1]

    units = [(i, j) for i in range(n_tiles) for j in range(HEADS_PER_TILE)]
    pending = [scores(*u) for u in units[:SCORE_LOOKAHEAD]]
    outs = []
    for n, (i, j) in enumerate(units):
        if n + SCORE_LOOKAHEAD < len(units):
            pending.append(scores(*units[n + SCORE_LOOKAHEAD]))
        outs.append(attend(i, j, pending.pop(0)))
        if len(outs) == HEADS_PER_TILE:
            o_ref[0, 0, i * Q_TILE:(i + 1) * Q_TILE, :] = jnp.concatenate(outs, axis=0).T.astype(BF16)
            outs = []


def _fox_attn(q, k, dec, vt):
    b, n_pairs, s, _ = q.shape
    tile = pl.BlockSpec((1, 1, s, LANES), lambda i, j: (i, j, 0, 0))
    return pl.pallas_call(
        _fox_attn_kernel,
        grid=(b, n_pairs),
        in_specs=[tile, tile, tile,
                  pl.BlockSpec((1, HEADS_PER_TILE * V_ROWS, s), lambda i, j: (i, j, 0))],
        out_specs=tile,
        out_shape=jax.ShapeDtypeStruct(q.shape, BF16),
        compiler_params=_params(("parallel", "parallel")),
        name="fox_attention",
    )(q, k, dec, vt)


def kernel(x, norm_mix_g, norm_ffn_g, a_w_in, a_ln_g, a_ln_b, a_w_s, a_b_s, a_w_out,
           b_w_in, b_f_bias, b_q_norm_g, b_k_norm_g, b_w_out,
           ffn_w_gate, ffn_w_up, ffn_w_down):
    bsz, seq, d = x.shape
    row = lambda p: p.reshape(1, -1).astype(F32)
    bf = lambda w: w.astype(BF16)

    b_s_tile = jnp.repeat(a_b_s[0].T, LANES, axis=1)
    h = _gmlp_layer(x.reshape(bsz * seq, d), row(norm_mix_g[0]), bf(a_w_in[0]),
                    row(a_ln_g[0]), row(a_ln_b[0]), a_w_s[0], b_s_tile, bf(a_w_out[0]))
    ffn_w = (bf(ffn_w_gate), bf(ffn_w_up), bf(ffn_w_down))
    h = _ffn_layer(h, 0, row(norm_ffn_g[0]), *ffn_w)

    w_in = bf(b_w_in[0])
    wf_pad = jnp.pad(w_in[:, 4 * d:], ((0, 0), (0, LANES - N_HEADS)))
    fb_pad = jnp.pad(b_f_bias[0], (0, LANES - N_HEADS)).reshape(1, LANES)
    qg_tile = jnp.tile(b_q_norm_g[0], HEADS_PER_TILE).reshape(1, LANES)
    kg_tile = jnp.tile(b_k_norm_g[0], HEADS_PER_TILE).reshape(1, LANES)
    q, k, dec, vt, gate = _fox_proj(h.reshape(bsz, seq, d), row(norm_mix_g[1]),
                                    w_in, wf_pad, fb_pad, qg_tile, kg_tile)
    attn = _fox_attn(q, k, dec, vt)
    h = _ffn_layer(h, 1, row(norm_ffn_g[1]), *ffn_w, attn=attn, gate=gate, w_o=bf(b_w_out[0]))
    return h.reshape(bsz, seq, d)
```

```python
import functools
import math

import jax
import jax.numpy as jnp
from jax import lax
from jax.experimental import pallas as pl
from jax.experimental.pallas import tpu as pltpu

F32 = jnp.float32
BF16 = jnp.bfloat16

EPS = 1e-6
CHUNK = 64
GMLP_BLOCK = 128
GMLP_GROUPS = 8
N_HEADS = 16
HEAD_DIM = 64
LANES = 128
SUBLANES = 8
BF16_SUBLANES = 16
HEADS_PER_TILE = LANES // HEAD_DIM
V_ROWS = HEAD_DIM + BF16_SUBLANES
LOG2E = math.log2(math.e)

ROW_TILE = 512
GMLP_ROW_TILE = 1024
GMLP_SUB_TILE = 256
GMLP_LOOKAHEAD = 1
FFN_ROW_TILE = 512
Q_TILE = 256
SCORE_LOOKAHEAD = 4
ATTN_PAIRS_PER_STEP = 2
VMEM_LIMIT_BYTES = 56 * 1024 * 1024
LOGIT_LIMIT = 64.0


def _const_spec(shape):
    nd = len(shape)
    return pl.BlockSpec(shape, lambda *_: (0,) * nd, pipeline_mode=pl.Buffered(1))


def _params(semantics):
    return pltpu.CompilerParams(dimension_semantics=semantics,
                                vmem_limit_bytes=VMEM_LIMIT_BYTES)


def _rms_norm(x, g):
    ms = jnp.mean(x * x, axis=-1, keepdims=True)
    return x * lax.rsqrt(ms + EPS) * g


def _dot(a, b):
    return jnp.dot(a, b, preferred_element_type=F32)


def _gelu_exact(x):
    return 0.5 * x * (1.0 + lax.erf(x * math.sqrt(0.5)))


def _split3(x):
    hi = x.astype(BF16).astype(F32)
    r = x - hi
    mid = r.astype(BF16).astype(F32)
    lo = (r - mid).astype(BF16).astype(F32)
    return hi, mid, lo


def _gmlp_kernel(x_ref, g_ref, win_ref, lng_ref, lnb_ref, ws_ref, bs_ref, wout_ref, o_ref):
    d = x_ref.shape[-1]
    t_chunk = lax.broadcasted_iota(jnp.int32, (GMLP_BLOCK, GMLP_BLOCK), 0) // CHUNK
    s_chunk = lax.broadcasted_iota(jnp.int32, (GMLP_BLOCK, GMLP_BLOCK), 1) // CHUNK
    visible = s_chunk <= t_chunk
    ws = [jnp.where(visible, ws_ref[g], 0.0).astype(BF16) for g in range(GMLP_GROUPS)]
    bias = bs_ref[...]

    def expand(r):
        x = x_ref[r * GMLP_SUB_TILE:(r + 1) * GMLP_SUB_TILE, :]
        return _dot(_rms_norm(x, g_ref[...]).astype(BF16), win_ref[...])

    def mix(r, z):
        z = _gelu_exact(z)
        u = z[:, :d]
        v = z[:, d:]
        mu = jnp.mean(v, axis=-1, keepdims=True)
        vc = v - mu
        var = jnp.mean(vc * vc, axis=-1, keepdims=True)
        vn = (vc * lax.rsqrt(var + EPS) * lng_ref[...] + lnb_ref[...]).astype(BF16)
        blocks = []
        for n in range(GMLP_SUB_TILE // GMLP_BLOCK):
            vb = vn[n * GMLP_BLOCK:(n + 1) * GMLP_BLOCK]
            cols = [_dot(ws[g], vb[:, g * LANES:(g + 1) * LANES]) for g in range(GMLP_GROUPS)]
            blocks.append(jnp.concatenate(cols, axis=1) + bias)
        gated = (u * jnp.concatenate(blocks, axis=0)).astype(BF16)
        rows = slice(r * GMLP_SUB_TILE, (r + 1) * GMLP_SUB_TILE)
        o_ref[rows, :] = x_ref[rows, :] + _dot(gated, wout_ref[...])

    n_sub = x_ref.shape[0] // GMLP_SUB_TILE
    pending = [expand(r) for r in range(min(GMLP_LOOKAHEAD, n_sub))]
    for r in range(n_sub):
        if r + GMLP_LOOKAHEAD < n_sub:
            pending.append(expand(r + GMLP_LOOKAHEAD))
        mix(r, pending.pop(0))


def _gmlp_layer(x2d, g, w_in, ln_g, ln_b, w_s, b_s_tile, w_out):
    t, d = x2d.shape
    row = pl.BlockSpec((GMLP_ROW_TILE, d), lambda i: (i, 0))
    return pl.pallas_call(
        _gmlp_kernel,
        grid=(t // GMLP_ROW_TILE,),
        in_specs=[row, _const_spec(g.shape), _const_spec(w_in.shape), _const_spec(ln_g.shape),
                  _const_spec(ln_b.shape), _const_spec(w_s.shape), _const_spec(b_s_tile.shape),
                  _const_spec(w_out.shape)],
        out_specs=row,
        out_shape=jax.ShapeDtypeStruct((t, d), F32),
        compiler_params=_params(("parallel",)),
        name="gmlp_mixer",
    )(x2d, g, w_in, ln_g, ln_b, w_s, b_s_tile, w_out)


def _swiglu(h, g, wg_ref, wu_ref, wd_ref):
    hn = _rms_norm(h, g).astype(BF16)
    a = _dot(hn, wg_ref[...])
    b = _dot(hn, wu_ref[...])
    act = (a * jax.nn.sigmoid(a) * b).astype(BF16)
    return h + _dot(act, wd_ref[...])


def _ffn_kernel(h_ref, g_ref, wg_ref, wu_ref, wd_ref, o_ref):
    o_ref[...] = _swiglu(h_ref[...], g_ref[...], wg_ref, wu_ref, wd_ref)


def _load_head_pairs(ref):
    return jnp.concatenate([ref[0, p] for p in range(ref.shape[1])], axis=1).astype(F32)


def _proj_ffn_kernel(h_ref, a_ref, gate_ref, wo_ref, g_ref, wg_ref, wu_ref, wd_ref, o_ref):
    a = (_load_head_pairs(a_ref) * _load_head_pairs(gate_ref)).astype(BF16)
    h = h_ref[...] + _dot(a, wo_ref[...])
    o_ref[...] = _swiglu(h, g_ref[...], wg_ref, wu_ref, wd_ref)


def _layer_spec(stacked, layer):
    nd = stacked.ndim - 1
    return pl.BlockSpec((None,) + stacked.shape[1:], lambda *_: (layer,) + (0,) * nd,
                        pipeline_mode=pl.Buffered(1))


def _ffn_layer(h2d, layer, g, w_gate, w_up, w_down, attn=None, gate=None, w_o=None):
    t, d = h2d.shape
    tm = FFN_ROW_TILE
    row = pl.BlockSpec((tm, d), lambda i: (i, 0))
    weights = [g, w_gate, w_up, w_down]
    weight_specs = [_const_spec(g.shape)] + [_layer_spec(w, layer) for w in weights[1:]]
    if attn is None:
        kernel, args, specs, name = _ffn_kernel, [h2d], [row], "swiglu_ffn"
    else:
        tiles_per_seq = attn.shape[2] // tm
        pairs = pl.BlockSpec((1, attn.shape[1], tm, LANES),
                             lambda i: (i // tiles_per_seq, 0, i % tiles_per_seq, 0))
        kernel, args, specs, name = _proj_ffn_kernel, [h2d, attn, gate, w_o], \
            [row, pairs, pairs, _const_spec(w_o.shape)], "attn_out_swiglu_ffn"
    return pl.pallas_call(
        kernel,
        grid=(t // tm,),
        in_specs=specs + weight_specs,
        out_specs=row,
        out_shape=jax.ShapeDtypeStruct((t, d), F32),
        compiler_params=_params(("parallel",)),
        name=name,
    )(*args, *weights)


DECAY_PARTS = 3


def _head_norm(x, gain, first_half):
    sq = x * x
    ss_a = jnp.sum(jnp.where(first_half, sq, 0.0), axis=-1, keepdims=True)
    ss_b = jnp.sum(jnp.where(first_half, 0.0, sq), axis=-1, keepdims=True)
    inv = lax.rsqrt(jnp.where(first_half, ss_a, ss_b) * (1.0 / HEAD_DIM) + EPS)
    return x * inv * gain


def _store_head_pairs(ref, x, fn=None):
    for p in range(ref.shape[1]):
        tile = x[:, p * LANES:(p + 1) * LANES]
        ref[0, p] = (tile if fn is None else fn(tile)).astype(ref.dtype)


def _decay_rows(parts, head):
    tm = parts[0].shape[1]
    sub = lax.broadcasted_iota(jnp.int32, (SUBLANES, tm), 0)
    rows = jnp.zeros((SUBLANES, tm), F32)
    for n, part in enumerate(parts):
        r = part[head:head + 1, :]
        rows = jnp.where(sub == n, r, jnp.where(sub == n + DECAY_PARTS, -r, rows))
    return rows


def _fox_proj_kernel(h_ref, g_ref, win_ref, wf_ref, fb_ref, qg_ref, kg_ref,
                     q_ref, k_ref, dec_ref, vt_ref, gate_ref, carry_ref):
    @pl.when(pl.program_id(1) == 0)
    def _():
        carry_ref[...] = jnp.zeros_like(carry_ref)

    d = h_ref.shape[-1]
    wq_ref, wk_ref, wv_ref, wg_ref = (win_ref.at[:, n * d:(n + 1) * d] for n in range(4))
    first_half = lax.broadcasted_iota(jnp.int32, (1, LANES), 1) < HEAD_DIM
    q_gain = qg_ref[...] * (HEAD_DIM ** -0.5 * LOG2E)
    k_gain = kg_ref[...]
    hn = _rms_norm(h_ref[0], g_ref[...]).astype(BF16)

    log_f = jax.nn.log_sigmoid(_dot(hn, wf_ref[...]) + fb_ref[...])
    q = _dot(hn, wq_ref[...])
    tm = log_f.shape[0]
    lower = (lax.broadcasted_iota(jnp.int32, (tm, tm), 0)
             >= lax.broadcasted_iota(jnp.int32, (tm, tm), 1)).astype(BF16)
    hi, mid, lo = _split3(log_f)
    cum = carry_ref[...] + (_dot(lower, hi.astype(BF16)) + _dot(lower, mid.astype(BF16))
                            + _dot(lower, lo.astype(BF16)))
    carry_ref[...] = cum[tm - 1:tm, :]
    k = _dot(hn, wk_ref[...])
    v = _dot(hn, wv_ref[...])
    g = _dot(hn, wg_ref[...])

    _store_head_pairs(q_ref, q, lambda t: _head_norm(t, q_gain, first_half))
    parts = _split3(cum.T[:N_HEADS, :] * LOG2E)
    gap = jnp.zeros((HEAD_DIM - SUBLANES, tm), F32)
    for p in range(dec_ref.shape[1]):
        slab = jnp.concatenate([_decay_rows(parts, HEADS_PER_TILE * p + 1), gap,
                                _decay_rows(parts, HEADS_PER_TILE * p), gap], axis=0)
        dec_ref[0, p] = slab.T.astype(BF16)
    _store_head_pairs(k_ref, k, lambda t: _head_norm(t, k_gain, first_half))
    vt = v.T
    ones_rows = jnp.ones((BF16_SUBLANES, tm), F32)
    vt_ref[0] = jnp.concatenate(
        [piece for h in range(N_HEADS) for piece in (vt[h * HEAD_DIM:(h + 1) * HEAD_DIM], ones_rows)],
        axis=0).astype(BF16)
    _store_head_pairs(gate_ref, jax.nn.sigmoid(g))


def _fox_proj(h3d, g, w_in, wf_pad, fb_pad, qg_tile, kg_tile):
    b, s, d = h3d.shape
    row = pl.BlockSpec((1, ROW_TILE, d), lambda i, j: (i, j, 0))
    pairs = pl.BlockSpec((1, d // LANES, ROW_TILE, LANES), lambda i, j: (i, 0, j, 0))
    bf = jax.ShapeDtypeStruct((b, d // LANES, s, LANES), BF16)
    vt_rows = N_HEADS * V_ROWS
    consts = (g, w_in, wf_pad, fb_pad, qg_tile, kg_tile)
    return pl.pallas_call(
        _fox_proj_kernel,
        grid=(b, s // ROW_TILE),
        in_specs=[row] + [_const_spec(w.shape) for w in consts],
        out_specs=[pairs, pairs, pairs, pl.BlockSpec((1, vt_rows, ROW_TILE), lambda i, j: (i, 0, j)), pairs],
        out_shape=[bf, bf, bf, jax.ShapeDtypeStruct((b, vt_rows, s), BF16), bf],
        scratch_shapes=[pltpu.VMEM((1, LANES), F32)],
        compiler_params=_params(("parallel", "arbitrary")),
        name="fox_proj",
    )(h3d, *consts)


def _fox_attn_kernel(q_ref, k_ref, dec_ref, vt_ref, o_ref, *, bounded):
    s_len = q_ref.shape[2]
    lane = lax.broadcasted_iota(jnp.int32, (Q_TILE, LANES), 1).astype(F32).astype(BF16)
    one = jnp.ones((Q_TILE, LANES), BF16)
    visible = (lax.broadcasted_iota(jnp.int32, (Q_TILE, Q_TILE), 0)
               <= lax.broadcasted_iota(jnp.int32, (Q_TILE, Q_TILE), 1))
    nt = (((1,), (1,)), ((), ()))

    def with_decay(x, dec, j, ones_first):
        base = HEAD_DIM * (1 - j)
        start = base if ones_first else base + DECAY_PARTS
        aug = jnp.where(lane >= start, jnp.where(lane < start + DECAY_PARTS, one, dec), dec)
        own = (lane < HEAD_DIM) if j == 0 else (lane >= HEAD_DIM)
        return jnp.where(own, x, aug)

    n_tiles = s_len // Q_TILE
    n_pairs = q_ref.shape[1]
    rows = lambda ref, p, i: ref[0, p, i * Q_TILE:(i + 1) * Q_TILE, :]
    k_heads = {(p, j): jnp.concatenate([with_decay(rows(k_ref, p, i), rows(dec_ref, p, i), j, ones_first=True)
                                        for i in range(n_tiles)], axis=0)
               for p in range(n_pairs) for j in range(HEADS_PER_TILE)}

    def scores(p, i, j):
        qh = with_decay(rows(q_ref, p, i), rows(dec_ref, p, i), j, ones_first=False)
        return lax.dot_general(k_heads[p, j][:(i + 1) * Q_TILE], qh, nt, preferred_element_type=F32)

    def attend(p, i, j, st):
        q0, q1 = i * Q_TILE, (i + 1) * Q_TILE
        diag = jnp.where(visible, st[q0:], -jnp.inf)
        past = st[:q0] if i else None
        if not bounded:
            m = jnp.max(diag, axis=0, keepdims=True)
            if i:
                m = jnp.maximum(m, jnp.max(past, axis=0, keepdims=True))
                past = past - m
            diag = diag - m
        pt = jnp.exp2(diag).astype(BF16)
        if i:
            pt = jnp.concatenate([jnp.exp2(past).astype(BF16), pt], axis=0)
        v0 = (p * HEADS_PER_TILE + j) * V_ROWS
        acc = _dot(vt_ref[0, v0:v0 + V_ROWS, :q1], pt)
        return acc[:HEAD_DIM] * (1.0 / acc[HEAD_DIM:HEAD_DIM + 1])

    units = [(p, i, j) for p in range(n_pairs) for i in range(n_tiles) for j in range(HEADS_PER_TILE)]
    pending = [scores(*u) for u in units[:SCORE_LOOKAHEAD]]
    outs = {}
    for n, (p, i, j) in enumerate(units):
        if n + SCORE_LOOKAHEAD < len(units):
            pending.append(scores(*units[n + SCORE_LOOKAHEAD]))
        outs[j] = attend(p, i, j, pending.pop(0))
        if len(outs) == HEADS_PER_TILE:
            o_t = jnp.concatenate([outs[h] for h in range(HEADS_PER_TILE)], axis=0)
            o_ref[0, p, i * Q_TILE:(i + 1) * Q_TILE, :] = o_t.T.astype(BF16)
            outs = {}


def _fox_attn(q, k, dec, vt, *, bounded):
    b, n_pairs, s, _ = q.shape
    pp = ATTN_PAIRS_PER_STEP
    tile = pl.BlockSpec((1, pp, s, LANES), lambda i, j: (i, j, 0, 0))
    return pl.pallas_call(
        functools.partial(_fox_attn_kernel, bounded=bounded),
        grid=(b, n_pairs // pp),
        in_specs=[tile, tile, tile,
                  pl.BlockSpec((1, pp * HEADS_PER_TILE * V_ROWS, s), lambda i, j: (i, j, 0))],
        out_specs=tile,
        out_shape=jax.ShapeDtypeStruct(q.shape, BF16),
        compiler_params=_params(("parallel", "parallel")),
        name="fox_attention_bounded" if bounded else "fox_attention",
    )(q, k, dec, vt)


def _logit_bound(qg, kg):
    return (HEAD_DIM ** 0.5 * LOG2E * 1.02) * jnp.max(jnp.abs(qg)) * jnp.max(jnp.abs(kg))


def kernel(x, norm_mix_g, norm_ffn_g, a_w_in, a_ln_g, a_ln_b, a_w_s, a_b_s, a_w_out,
           b_w_in, b_f_bias, b_q_norm_g, b_k_norm_g, b_w_out,
           ffn_w_gate, ffn_w_up, ffn_w_down):
    bsz, seq, d = x.shape
    row = lambda p: p.reshape(1, -1).astype(F32)
    bf = lambda w: w.astype(BF16)

    b_s_tile = jnp.repeat(a_b_s[0].T, LANES, axis=1)
    h = _gmlp_layer(x.reshape(bsz * seq, d), row(norm_mix_g[0]), bf(a_w_in[0]),
                    row(a_ln_g[0]), row(a_ln_b[0]), a_w_s[0], b_s_tile, bf(a_w_out[0]))
    ffn_w = (bf(ffn_w_gate), bf(ffn_w_up), bf(ffn_w_down))
    h = _ffn_layer(h, 0, row(norm_ffn_g[0]), *ffn_w)

    w_in = bf(b_w_in[0])
    wf_pad = jnp.pad(w_in[:, 4 * d:], ((0, 0), (0, LANES - N_HEADS)))
    fb_pad = jnp.pad(b_f_bias[0], (0, LANES - N_HEADS)).reshape(1, LANES)
    qg_tile = jnp.tile(b_q_norm_g[0], HEADS_PER_TILE).reshape(1, LANES)
    kg_tile = jnp.tile(b_k_norm_g[0], HEADS_PER_TILE).reshape(1, LANES)
    q, k, dec, vt, gate = _fox_proj(h.reshape(bsz, seq, d), row(norm_mix_g[1]),
                                    w_in, wf_pad, fb_pad, qg_tile, kg_tile)
    attn = lax.cond(_logit_bound(b_q_norm_g[0], b_k_norm_g[0]) <= LOGIT_LIMIT,
                    functools.partial(_fox_attn, bounded=True),
                    functools.partial(_fox_attn, bounded=False), q, k, dec, vt)
    h = _ffn_layer(h, 1, row(norm_ffn_g[1]), *ffn_w, attn=attn, gate=gate, w_o=bf(b_w_out[0]))
    return h.reshape(bsz, seq, d)
```

```python
import functools
import math

import jax
import jax.numpy as jnp
from jax import lax
from jax.experimental import pallas as pl
from jax.experimental.pallas import tpu as pltpu

F32 = jnp.float32
BF16 = jnp.bfloat16

EPS = 1e-6
CHUNK = 64
GMLP_BLOCK = 128
GMLP_GROUPS = 8
N_HEADS = 16
HEAD_DIM = 64
LANES = 128
SUBLANES = 8
BF16_SUBLANES = 16
HEADS_PER_TILE = LANES // HEAD_DIM
V_ROWS = HEAD_DIM + BF16_SUBLANES
LOG2E = math.log2(math.e)

ROW_TILE = 1024
PROJ_SUB_TILE = 256
GMLP_ROW_TILE = 1024
GMLP_SUB_TILE = 256
GMLP_LOOKAHEAD = 1
FFN_ROW_TILE = 1024
FFN_SUB_TILE = 256
Q_TILE = 256
SCORE_LOOKAHEAD = 4
ATTN_PAIRS_PER_STEP = 2
VMEM_LIMIT_BYTES = 56 * 1024 * 1024
LOGIT_LIMIT = 64.0


def _const_spec(shape):
    nd = len(shape)
    return pl.BlockSpec(shape, lambda *_: (0,) * nd, pipeline_mode=pl.Buffered(1))


def _params(semantics):
    return pltpu.CompilerParams(dimension_semantics=semantics,
                                vmem_limit_bytes=VMEM_LIMIT_BYTES)


def _rms_norm(x, g):
    ms = jnp.mean(x * x, axis=-1, keepdims=True)
    return x * lax.rsqrt(ms + EPS) * g


def _dot(a, b):
    return jnp.dot(a, b, preferred_element_type=F32)


def _gelu_exact(x):
    return 0.5 * x * (1.0 + lax.erf(x * math.sqrt(0.5)))


def _split3(x):
    hi = x.astype(BF16).astype(F32)
    r = x - hi
    mid = r.astype(BF16).astype(F32)
    lo = (r - mid).astype(BF16).astype(F32)
    return hi, mid, lo


def _gmlp_kernel(x_ref, g_ref, win_ref, lng_ref, lnb_ref, ws_ref, bs_ref, wout_ref, o_ref):
    d = x_ref.shape[-1]
    t_chunk = lax.broadcasted_iota(jnp.int32, (GMLP_BLOCK, GMLP_BLOCK), 0) // CHUNK
    s_chunk = lax.broadcasted_iota(jnp.int32, (GMLP_BLOCK, GMLP_BLOCK), 1) // CHUNK
    visible = s_chunk <= t_chunk
    ws = [jnp.where(visible, ws_ref[g], 0.0).astype(BF16) for g in range(GMLP_GROUPS)]
    bias = bs_ref[...]

    def expand(r):
        x = x_ref[r * GMLP_SUB_TILE:(r + 1) * GMLP_SUB_TILE, :]
        return _dot(_rms_norm(x, g_ref[...]).astype(BF16), win_ref[...])

    def mix(r, z):
        z = _gelu_exact(z)
        u = z[:, :d]
        v = z[:, d:]
        mu = jnp.mean(v, axis=-1, keepdims=True)
        vc = v - mu
        var = jnp.mean(vc * vc, axis=-1, keepdims=True)
        vn = (vc * lax.rsqrt(var + EPS) * lng_ref[...] + lnb_ref[...]).astype(BF16)
        blocks = []
        for n in range(GMLP_SUB_TILE // GMLP_BLOCK):
            vb = vn[n * GMLP_BLOCK:(n + 1) * GMLP_BLOCK]
            cols = [_dot(ws[g], vb[:, g * LANES:(g + 1) * LANES]) for g in range(GMLP_GROUPS)]
            blocks.append(jnp.concatenate(cols, axis=1) + bias)
        gated = (u * jnp.concatenate(blocks, axis=0)).astype(BF16)
        rows = slice(r * GMLP_SUB_TILE, (r + 1) * GMLP_SUB_TILE)
        o_ref[rows, :] = x_ref[rows, :] + _dot(gated, wout_ref[...])

    n_sub = x_ref.shape[0] // GMLP_SUB_TILE
    pending = [expand(r) for r in range(min(GMLP_LOOKAHEAD, n_sub))]
    for r in range(n_sub):
        if r + GMLP_LOOKAHEAD < n_sub:
            pending.append(expand(r + GMLP_LOOKAHEAD))
        mix(r, pending.pop(0))


def _gmlp_layer(x2d, g, w_in, ln_g, ln_b, w_s, b_s_tile, w_out):
    t, d = x2d.shape
    row = pl.BlockSpec((GMLP_ROW_TILE, d), lambda i: (i, 0))
    return pl.pallas_call(
        _gmlp_kernel,
        grid=(t // GMLP_ROW_TILE,),
        in_specs=[row, _const_spec(g.shape), _const_spec(w_in.shape), _const_spec(ln_g.shape),
                  _const_spec(ln_b.shape), _const_spec(w_s.shape), _const_spec(b_s_tile.shape),
                  _const_spec(w_out.shape)],
        out_specs=row,
        out_shape=jax.ShapeDtypeStruct((t, d), F32),
        compiler_params=_params(("parallel",)),
        name="gmlp_mixer",
    )(x2d, g, w_in, ln_g, ln_b, w_s, b_s_tile, w_out)


def _swiglu_pipelined(residual, g_ref, wg_ref, wu_ref, wd_ref, o_ref):
    def expand(h):
        hn = _rms_norm(h, g_ref[...]).astype(BF16)
        return _dot(hn, wg_ref[...]), _dot(hn, wu_ref[...])

    def contract(r, h, a, b):
        act = (a * jax.nn.sigmoid(a) * b).astype(BF16)
        o_ref[r * FFN_SUB_TILE:(r + 1) * FFN_SUB_TILE, :] = h + _dot(act, wd_ref[...])

    n_sub = o_ref.shape[0] // FFN_SUB_TILE
    h = residual(0)
    a, b = expand(h)
    for r in range(n_sub):
        h_next = residual(r + 1) if r + 1 < n_sub else None
        contract(r, h, a, b)
        if h_next is not None:
            h = h_next
            a, b = expand(h)


def _ffn_kernel(h_ref, g_ref, wg_ref, wu_ref, wd_ref, o_ref):
    residual = lambda r: h_ref[r * FFN_SUB_TILE:(r + 1) * FFN_SUB_TILE, :]
    _swiglu_pipelined(residual, g_ref, wg_ref, wu_ref, wd_ref, o_ref)


def _load_head_pairs(ref, rows):
    return jnp.concatenate([ref[0, p, rows, :] for p in range(ref.shape[1])], axis=1).astype(F32)


def _proj_ffn_kernel(h_ref, a_ref, gate_ref, wo_ref, g_ref, wg_ref, wu_ref, wd_ref, o_ref):
    def residual(r):
        rows = slice(r * FFN_SUB_TILE, (r + 1) * FFN_SUB_TILE)
        a = (_load_head_pairs(a_ref, rows) * _load_head_pairs(gate_ref, rows)).astype(BF16)
        return h_ref[rows, :] + _dot(a, wo_ref[...])
    _swiglu_pipelined(residual, g_ref, wg_ref, wu_ref, wd_ref, o_ref)


def _layer_spec(stacked, layer):
    nd = stacked.ndim - 1
    return pl.BlockSpec((None,) + stacked.shape[1:], lambda *_: (layer,) + (0,) * nd,
                        pipeline_mode=pl.Buffered(1))


def _ffn_layer(h2d, layer, g, w_gate, w_up, w_down, attn=None, gate=None, w_o=None):
    t, d = h2d.shape
    tm = FFN_ROW_TILE
    row = pl.BlockSpec((tm, d), lambda i: (i, 0))
    weights = [g, w_gate, w_up, w_down]
    weight_specs = [_const_spec(g.shape)] + [_layer_spec(w, layer) for w in weights[1:]]
    if attn is None:
        kernel, args, specs, name = _ffn_kernel, [h2d], [row], "swiglu_ffn"
    else:
        tiles_per_seq = attn.shape[2] // tm
        pairs = pl.BlockSpec((1, attn.shape[1], tm, LANES),
                             lambda i: (i // tiles_per_seq, 0, i % tiles_per_seq, 0))
        kernel, args, specs, name = _proj_ffn_kernel, [h2d, attn, gate, w_o], \
            [row, pairs, pairs, _const_spec(w_o.shape)], "attn_out_swiglu_ffn"
    return pl.pallas_call(
        kernel,
        grid=(t // tm,),
        in_specs=specs + weight_specs,
        out_specs=row,
        out_shape=jax.ShapeDtypeStruct((t, d), F32),
        compiler_params=_params(("parallel",)),
        name=name,
    )(*args, *weights)


DECAY_PARTS = 3


def _head_norm(x, gain, first_half):
    sq = x * x
    ss_a = jnp.sum(jnp.where(first_half, sq, 0.0), axis=-1, keepdims=True)
    ss_b = jnp.sum(jnp.where(first_half, 0.0, sq), axis=-1, keepdims=True)
    inv = lax.rsqrt(jnp.where(first_half, ss_a, ss_b) * (1.0 / HEAD_DIM) + EPS)
    return x * inv * gain


def _store_head_pairs(ref, rows, x, fn=None):
    for p in range(ref.shape[1]):
        tile = x[:, p * LANES:(p + 1) * LANES]
        ref[0, p, rows, :] = (tile if fn is None else fn(tile)).astype(ref.dtype)


def _decay_rows(parts, head):
    tm = parts[0].shape[1]
    sub = lax.broadcasted_iota(jnp.int32, (SUBLANES, tm), 0)
    rows = jnp.zeros((SUBLANES, tm), F32)
    for n, part in enumerate(parts):
        r = part[head:head + 1, :]
        rows = jnp.where(sub == n, r, jnp.where(sub == n + DECAY_PARTS, -r, rows))
    return rows


def _fox_proj_kernel(h_ref, g_ref, win_ref, wf_ref, fb_ref, qg_ref, kg_ref,
                     q_ref, k_ref, dec_ref, vt_ref, gate_ref, carry_ref):
    @pl.when(pl.program_id(1) == 0)
    def _():
        carry_ref[...] = jnp.zeros_like(carry_ref)

    d = h_ref.shape[-1]
    tm = PROJ_SUB_TILE
    wq_ref, wk_ref, wv_ref, wg_ref = (win_ref.at[:, n * d:(n + 1) * d] for n in range(4))
    first_half = lax.broadcasted_iota(jnp.int32, (1, LANES), 1) < HEAD_DIM
    q_gain = qg_ref[...] * (HEAD_DIM ** -0.5 * LOG2E)
    k_gain = kg_ref[...]
    lower = (lax.broadcasted_iota(jnp.int32, (tm, tm), 0)
             >= lax.broadcasted_iota(jnp.int32, (tm, tm), 1)).astype(BF16)
    gap = jnp.zeros((HEAD_DIM - SUBLANES, tm), F32)
    ones_rows = jnp.ones((BF16_SUBLANES, tm), F32)

    for r in range(h_ref.shape[1] // tm):
        rows = slice(r * tm, (r + 1) * tm)
        hn = _rms_norm(h_ref[0, rows, :], g_ref[...]).astype(BF16)

        log_f = jax.nn.log_sigmoid(_dot(hn, wf_ref[...]) + fb_ref[...])
        q = _dot(hn, wq_ref[...])
        hi, mid, lo = _split3(log_f)
        cum = carry_ref[...] + (_dot(lower, hi.astype(BF16)) + _dot(lower, mid.astype(BF16))
                                + _dot(lower, lo.astype(BF16)))
        carry_ref[...] = cum[tm - 1:tm, :]
        k = _dot(hn, wk_ref[...])
        v = _dot(hn, wv_ref[...])
        g = _dot(hn, wg_ref[...])

        _store_head_pairs(q_ref, rows, q, lambda t: _head_norm(t, q_gain, first_half))
        parts = _split3(cum.T[:N_HEADS, :] * LOG2E)
        for p in range(dec_ref.shape[1]):
            slab = jnp.concatenate([_decay_rows(parts, HEADS_PER_TILE * p + 1), gap,
                                    _decay_rows(parts, HEADS_PER_TILE * p), gap], axis=0)
            dec_ref[0, p, rows, :] = slab.T.astype(BF16)
        _store_head_pairs(k_ref, rows, k, lambda t: _head_norm(t, k_gain, first_half))
        vt = v.T
        vt_ref[0, :, rows] = jnp.concatenate(
            [piece for h in range(N_HEADS) for piece in (vt[h * HEAD_DIM:(h + 1) * HEAD_DIM], ones_rows)],
            axis=0).astype(BF16)
        _store_head_pairs(gate_ref, rows, jax.nn.sigmoid(g))


def _fox_proj(h3d, g, w_in, wf_pad, fb_pad, qg_tile, kg_tile):
    b, s, d = h3d.shape
    row = pl.BlockSpec((1, ROW_TILE, d), lambda i, j: (i, j, 0))
    pairs = pl.BlockSpec((1, d // LANES, ROW_TILE, LANES), lambda i, j: (i, 0, j, 0))
    bf = jax.ShapeDtypeStruct((b, d // LANES, s, LANES), BF16)
    vt_rows = N_HEADS * V_ROWS
    consts = (g, w_in, wf_pad, fb_pad, qg_tile, kg_tile)
    return pl.pallas_call(
        _fox_proj_kernel,
        grid=(b, s // ROW_TILE),
        in_specs=[row] + [_const_spec(w.shape) for w in consts],
        out_specs=[pairs, pairs, pairs, pl.BlockSpec((1, vt_rows, ROW_TILE), lambda i, j: (i, 0, j)), pairs],
        out_shape=[bf, bf, bf, jax.ShapeDtypeStruct((b, vt_rows, s), BF16), bf],
        scratch_shapes=[pltpu.VMEM((1, LANES), F32)],
        compiler_params=_params(("parallel", "arbitrary")),
        name="fox_proj",
    )(h3d, *consts)


def _fox_attn_kernel(q_ref, k_ref, dec_ref, vt_ref, o_ref, *, bounded):
    s_len = q_ref.shape[2]
    lane = lax.broadcasted_iota(jnp.int32, (Q_TILE, LANES), 1).astype(F32).astype(BF16)
    one = jnp.ones((Q_TILE, LANES), BF16)
    visible = (lax.broadcasted_iota(jnp.int32, (Q_TILE, Q_TILE), 0)
               <= lax.broadcasted_iota(jnp.int32, (Q_TILE, Q_TILE), 1))
    nt = (((1,), (1,)), ((), ()))

    def with_decay(x, dec, j, ones_first):
        base = HEAD_DIM * (1 - j)
        start = base if ones_first else base + DECAY_PARTS
        aug = jnp.where(lane >= start, jnp.where(lane < start + DECAY_PARTS, one, dec), dec)
        own = (lane < HEAD_DIM) if j == 0 else (lane >= HEAD_DIM)
        return jnp.where(own, x, aug)

    n_tiles = s_len // Q_TILE
    n_pairs = q_ref.shape[1]
    rows = lambda ref, p, i: ref[0, p, i * Q_TILE:(i + 1) * Q_TILE, :]
    k_heads = {(p, j): jnp.concatenate([with_decay(rows(k_ref, p, i), rows(dec_ref, p, i), j, ones_first=True)
                                        for i in range(n_tiles)], axis=0)
               for p in range(n_pairs) for j in range(HEADS_PER_TILE)}

    def scores(p, i, j):
        qh = with_decay(rows(q_ref, p, i), rows(dec_ref, p, i), j, ones_first=False)
        return lax.dot_general(k_heads[p, j][:(i + 1) * Q_TILE], qh, nt, preferred_element_type=F32)

    def attend(p, i, j, st):
        q0, q1 = i * Q_TILE, (i + 1) * Q_TILE
        diag = jnp.where(visible, st[q0:], -jnp.inf)
        past = st[:q0] if i else None
        if not bounded:
            m = jnp.max(diag, axis=0, keepdims=True)
            if i:
                m = jnp.maximum(m, jnp.max(past, axis=0, keepdims=True))
                past = past - m
            diag = diag - m
        pt = jnp.exp2(diag).astype(BF16)
        if i:
            pt = jnp.concatenate([jnp.exp2(past).astype(BF16), pt], axis=0)
        v0 = (p * HEADS_PER_TILE + j) * V_ROWS
        acc = _dot(vt_ref[0, v0:v0 + V_ROWS, :q1], pt)
        return acc[:HEAD_DIM] * (1.0 / acc[HEAD_DIM:HEAD_DIM + 1])

    units = [(p, i, j) for p in range(n_pairs) for i in range(n_tiles) for j in range(HEADS_PER_TILE)]
    pending = [scores(*u) for u in units[:SCORE_LOOKAHEAD]]
    outs = {}
    for n, (p, i, j) in enumerate(units):
        if n + SCORE_LOOKAHEAD < len(units):
            pending.append(scores(*units[n + SCORE_LOOKAHEAD]))
        outs[j] = attend(p, i, j, pending.pop(0))
        if len(outs) == HEADS_PER_TILE:
            o_t = jnp.concatenate([outs[h] for h in range(HEADS_PER_TILE)], axis=0)
            o_ref[0, p, i * Q_TILE:(i + 1) * Q_TILE, :] = o_t.T.astype(BF16)
            outs = {}


def _fox_attn(q, k, dec, vt, *, bounded):
    b, n_pairs, s, _ = q.shape
    pp = ATTN_PAIRS_PER_STEP
    tile = pl.BlockSpec((1, pp, s, LANES), lambda i, j: (i, j, 0, 0))
    return pl.pallas_call(
        functools.partial(_fox_attn_kernel, bounded=bounded),
        grid=(b, n_pairs // pp),
        in_specs=[tile, tile, tile,
                  pl.BlockSpec((1, pp * HEADS_PER_TILE * V_ROWS, s), lambda i, j: (i, j, 0))],
        out_specs=tile,
        out_shape=jax.ShapeDtypeStruct(q.shape, BF16),
        compiler_params=_params(("parallel", "parallel")),
        name="fox_attention_bounded" if bounded else "fox_attention",
    )(q, k, dec, vt)


def _logit_bound(qg, kg):
    return (HEAD_DIM ** 0.5 * LOG2E * 1.02) * jnp.max(jnp.abs(qg)) * jnp.max(jnp.abs(kg))


def kernel(x, norm_mix_g, norm_ffn_g, a_w_in, a_ln_g, a_ln_b, a_w_s, a_b_s, a_w_out,
           b_w_in, b_f_bias, b_q_norm_g, b_k_norm_g, b_w_out,
           ffn_w_gate, ffn_w_up, ffn_w_down):
    bsz, seq, d = x.shape
    row = lambda p: p.reshape(1, -1).astype(F32)
    bf = lambda w: w.astype(BF16)

    b_s_tile = jnp.repeat(a_b_s[0].T, LANES, axis=1)
    h = _gmlp_layer(x.reshape(bsz * seq, d), row(norm_mix_g[0]), bf(a_w_in[0]),
                    row(a_ln_g[0]), row(a_ln_b[0]), a_w_s[0], b_s_tile, bf(a_w_out[0]))
    ffn_w = (bf(ffn_w_gate), bf(ffn_w_up), bf(ffn_w_down))
    h = _ffn_layer(h, 0, row(norm_ffn_g[0]), *ffn_w)

    w_in = bf(b_w_in[0])
    wf_pad = jnp.pad(w_in[:, 4 * d:], ((0, 0), (0, LANES - N_HEADS)))
    fb_pad = jnp.pad(b_f_bias[0], (0, LANES - N_HEADS)).reshape(1, LANES)
    qg_tile = jnp.tile(b_q_norm_g[0], HEADS_PER_TILE).reshape(1, LANES)
    kg_tile = jnp.tile(b_k_norm_g[0], HEADS_PER_TILE).reshape(1, LANES)
    q, k, dec, vt, gate = _fox_proj(h.reshape(bsz, seq, d), row(norm_mix_g[1]),
                                    w_in, wf_pad, fb_pad, qg_tile, kg_tile)
    attn = lax.cond(_logit_bound(b_q_norm_g[0], b_k_norm_g[0]) <= LOGIT_LIMIT,
                    functools.partial(_fox_attn, bounded=True),
                    functools.partial(_fox_attn, bounded=False), q, k, dec, vt)
    h = _ffn_layer(h, 1, row(norm_ffn_g[1]), *ffn_w, attn=attn, gate=gate, w_o=bf(b_w_out[0]))
    return h.reshape(bsz, seq, d)
```

```python
import functools
import math

import jax
import jax.numpy as jnp
from jax import lax
from jax.experimental import pallas as pl
from jax.experimental.pallas import tpu as pltpu

F32 = jnp.float32
BF16 = jnp.bfloat16

EPS = 1e-6
CHUNK = 64
GMLP_BLOCK = 128
GMLP_GROUPS = 8
N_HEADS = 16
HEAD_DIM = 64
LANES = 128
SUBLANES = 8
BF16_SUBLANES = 16
HEADS_PER_TILE = LANES // HEAD_DIM
V_ROWS = HEAD_DIM + BF16_SUBLANES
LOG2E = math.log2(math.e)

ROW_TILE = 1024
PROJ_SUB_TILE = 256
GMLP_ROW_TILE = 1024
GMLP_SUB_TILE = 256
FFN_ROW_TILE = 1024
FFN_SUB_TILE = 256
Q_TILE = 256
SCORE_LOOKAHEAD = 4
ATTN_PAIRS_PER_STEP = 2
ATTN_PAIRS_PER_STEP_BOUNDED = 4
VMEM_LIMIT_BYTES = 56 * 1024 * 1024
LOGIT_LIMIT = 64.0


def _const_spec(shape):
    nd = len(shape)
    return pl.BlockSpec(shape, lambda *_: (0,) * nd, pipeline_mode=pl.Buffered(1))


def _params(semantics):
    return pltpu.CompilerParams(dimension_semantics=semantics,
                                vmem_limit_bytes=VMEM_LIMIT_BYTES)


def _rms_norm(x, g):
    ms = jnp.mean(x * x, axis=-1, keepdims=True)
    return x * lax.rsqrt(ms + EPS) * g


def _dot(a, b):
    return jnp.dot(a, b, preferred_element_type=F32)


def _gelu_exact(x):
    return 0.5 * x * (1.0 + lax.erf(x * math.sqrt(0.5)))


def _split3(x):
    hi = x.astype(BF16).astype(F32)
    r = x - hi
    mid = r.astype(BF16).astype(F32)
    lo = (r - mid).astype(BF16).astype(F32)
    return hi, mid, lo


def _gmlp_kernel(x_ref, g_ref, win_ref, lng_ref, lnb_ref, ws_ref, bs_ref, wout_ref, o_ref):
    d = x_ref.shape[-1]
    t_chunk = lax.broadcasted_iota(jnp.int32, (GMLP_BLOCK, GMLP_BLOCK), 0) // CHUNK
    s_chunk = lax.broadcasted_iota(jnp.int32, (GMLP_BLOCK, GMLP_BLOCK), 1) // CHUNK
    visible = s_chunk <= t_chunk
    ws = [jnp.where(visible, ws_ref[g], 0.0).astype(BF16) for g in range(GMLP_GROUPS)]
    bias = bs_ref[...]

    def expand(r):
        x = x_ref[r * GMLP_SUB_TILE:(r + 1) * GMLP_SUB_TILE, :]
        return _dot(_rms_norm(x, g_ref[...]).astype(BF16), win_ref[...])

    def mix(z):
        z = _gelu_exact(z)
        u = z[:, :d]
        v = z[:, d:]
        mu = jnp.mean(v, axis=-1, keepdims=True)
        vc = v - mu
        var = jnp.mean(vc * vc, axis=-1, keepdims=True)
        vn = (vc * lax.rsqrt(var + EPS) * lng_ref[...] + lnb_ref[...]).astype(BF16)
        blocks = []
        for n in range(GMLP_SUB_TILE // GMLP_BLOCK):
            vb = vn[n * GMLP_BLOCK:(n + 1) * GMLP_BLOCK]
            cols = [_dot(ws[g], vb[:, g * LANES:(g + 1) * LANES]) for g in range(GMLP_GROUPS)]
            blocks.append(jnp.concatenate(cols, axis=1) + bias)
        return (u * jnp.concatenate(blocks, axis=0)).astype(BF16)

    def project(r, gated):
        rows = slice(r * GMLP_SUB_TILE, (r + 1) * GMLP_SUB_TILE)
        o_ref[rows, :] = x_ref[rows, :] + _dot(gated, wout_ref[...])

    n_sub = x_ref.shape[0] // GMLP_SUB_TILE
    z = {r: expand(r) for r in range(min(2, n_sub))}
    gated = {0: mix(z.pop(0))}
    for r in range(n_sub):
        if r + 2 < n_sub:
            z[r + 2] = expand(r + 2)
        if r + 1 < n_sub:
            gated[r + 1] = mix(z.pop(r + 1))
        project(r, gated.pop(r))


def _gmlp_layer(x2d, g, w_in, ln_g, ln_b, w_s, b_s_tile, w_out):
    t, d = x2d.shape
    row = pl.BlockSpec((GMLP_ROW_TILE, d), lambda i: (i, 0))
    return pl.pallas_call(
        _gmlp_kernel,
        grid=(t // GMLP_ROW_TILE,),
        in_specs=[row, _const_spec(g.shape), _const_spec(w_in.shape), _const_spec(ln_g.shape),
                  _const_spec(ln_b.shape), _const_spec(w_s.shape), _const_spec(b_s_tile.shape),
                  _const_spec(w_out.shape)],
        out_specs=row,
        out_shape=jax.ShapeDtypeStruct((t, d), F32),
        compiler_params=_params(("parallel",)),
        name="gmlp_mixer",
    )(x2d, g, w_in, ln_g, ln_b, w_s, b_s_tile, w_out)


def _swiglu_pipelined(residual, g_ref, wg_ref, wu_ref, wd_ref, o_ref):
    def expand(h):
        hn = _rms_norm(h, g_ref[...]).astype(BF16)
        return _dot(hn, wg_ref[...]), _dot(hn, wu_ref[...])

    def contract(r, h, a, b):
        act = (a * jax.nn.sigmoid(a) * b).astype(BF16)
        o_ref[r * FFN_SUB_TILE:(r + 1) * FFN_SUB_TILE, :] = h + _dot(act, wd_ref[...])

    n_sub = o_ref.shape[0] // FFN_SUB_TILE
    h = residual(0)
    a, b = expand(h)
    for r in range(n_sub):
        h_next = residual(r + 1) if r + 1 < n_sub else None
        contract(r, h, a, b)
        if h_next is not None:
            h = h_next
            a, b = expand(h)


def _ffn_kernel(h_ref, g_ref, wg_ref, wu_ref, wd_ref, o_ref):
    residual = lambda r: h_ref[r * FFN_SUB_TILE:(r + 1) * FFN_SUB_TILE, :]
    _swiglu_pipelined(residual, g_ref, wg_ref, wu_ref, wd_ref, o_ref)


def _load_head_pairs(ref, rows):
    return jnp.concatenate([ref[0, p, rows, :] for p in range(ref.shape[1])], axis=1).astype(F32)


def _proj_ffn_kernel(h_ref, a_ref, gate_ref, wo_ref, g_ref, wg_ref, wu_ref, wd_ref, o_ref):
    def residual(r):
        rows = slice(r * FFN_SUB_TILE, (r + 1) * FFN_SUB_TILE)
        a = (_load_head_pairs(a_ref, rows) * _load_head_pairs(gate_ref, rows)).astype(BF16)
        return h_ref[rows, :] + _dot(a, wo_ref[...])
    _swiglu_pipelined(residual, g_ref, wg_ref, wu_ref, wd_ref, o_ref)


def _layer_spec(stacked, layer):
    nd = stacked.ndim - 1
    return pl.BlockSpec((None,) + stacked.shape[1:], lambda *_: (layer,) + (0,) * nd,
                        pipeline_mode=pl.Buffered(1))


def _ffn_layer(h2d, layer, g, w_gate, w_up, w_down, attn=None, gate=None, w_o=None):
    t, d = h2d.shape
    tm = FFN_ROW_TILE
    row = pl.BlockSpec((tm, d), lambda i: (i, 0))
    weights = [g, w_gate, w_up, w_down]
    weight_specs = [_const_spec(g.shape)] + [_layer_spec(w, layer) for w in weights[1:]]
    if attn is None:
        kernel, args, specs, name = _ffn_kernel, [h2d], [row], "swiglu_ffn"
    else:
        tiles_per_seq = attn.shape[2] // tm
        pairs = pl.BlockSpec((1, attn.shape[1], tm, LANES),
                             lambda i: (i // tiles_per_seq, 0, i % tiles_per_seq, 0))
        kernel, args, specs, name = _proj_ffn_kernel, [h2d, attn, gate, w_o], \
            [row, pairs, pairs, _const_spec(w_o.shape)], "attn_out_swiglu_ffn"
    return pl.pallas_call(
        kernel,
        grid=(t // tm,),
        in_specs=specs + weight_specs,
        out_specs=row,
        out_shape=jax.ShapeDtypeStruct((t, d), F32),
        compiler_params=_params(("parallel",)),
        name=name,
    )(*args, *weights)


DECAY_PARTS = 3


def _head_norm(x, gain, first_half):
    sq = x * x
    ss_a = jnp.sum(jnp.where(first_half, sq, 0.0), axis=-1, keepdims=True)
    ss_b = jnp.sum(jnp.where(first_half, 0.0, sq), axis=-1, keepdims=True)
    inv = lax.rsqrt(jnp.where(first_half, ss_a, ss_b) * (1.0 / HEAD_DIM) + EPS)
    return x * inv * gain


def _store_head_pairs(ref, rows, x, fn=None):
    for p in range(ref.shape[1]):
        tile = x[:, p * LANES:(p + 1) * LANES]
        ref[0, p, rows, :] = (tile if fn is None else fn(tile)).astype(ref.dtype)


def _decay_rows(parts, head):
    tm = parts[0].shape[1]
    sub = lax.broadcasted_iota(jnp.int32, (SUBLANES, tm), 0)
    rows = jnp.zeros((SUBLANES, tm), F32)
    for n, part in enumerate(parts):
        r = part[head:head + 1, :]
        rows = jnp.where(sub == n, r, jnp.where(sub == n + DECAY_PARTS, -r, rows))
    return rows


def _fox_proj_kernel(h_ref, g_ref, win_ref, wf_ref, fb_ref, qg_ref, kg_ref,
                     q_ref, k_ref, dec_ref, vt_ref, gate_ref, carry_ref):
    @pl.when(pl.program_id(1) == 0)
    def _():
        carry_ref[...] = jnp.zeros_like(carry_ref)

    d = h_ref.shape[-1]
    tm = PROJ_SUB_TILE
    wq_ref, wk_ref, wv_ref, wg_ref = (win_ref.at[:, n * d:(n + 1) * d] for n in range(4))
    first_half = lax.broadcasted_iota(jnp.int32, (1, LANES), 1) < HEAD_DIM
    q_gain = qg_ref[...] * (HEAD_DIM ** -0.5 * LOG2E)
    k_gain = kg_ref[...]
    lower = (lax.broadcasted_iota(jnp.int32, (tm, tm), 0)
             >= lax.broadcasted_iota(jnp.int32, (tm, tm), 1)).astype(BF16)
    gap = jnp.zeros((HEAD_DIM - SUBLANES, tm), F32)
    ones_rows = jnp.ones((BF16_SUBLANES, tm), F32)

    for r in range(h_ref.shape[1] // tm):
        rows = slice(r * tm, (r + 1) * tm)
        hn = _rms_norm(h_ref[0, rows, :], g_ref[...]).astype(BF16)

        log_f = jax.nn.log_sigmoid(_dot(hn, wf_ref[...]) + fb_ref[...])
        q = _dot(hn, wq_ref[...])
        hi, mid, lo = _split3(log_f)
        cum = carry_ref[...] + (_dot(lower, hi.astype(BF16)) + _dot(lower, mid.astype(BF16))
                                + _dot(lower, lo.astype(BF16)))
        carry_ref[...] = cum[tm - 1:tm, :]
        k = _dot(hn, wk_ref[...])
        v = _dot(hn, wv_ref[...])
        g = _dot(hn, wg_ref[...])

        _store_head_pairs(q_ref, rows, q, lambda t: _head_norm(t, q_gain, first_half))
        parts = _split3(cum.T[:N_HEADS, :] * LOG2E)
        for p in range(dec_ref.shape[1]):
            slab = jnp.concatenate([_decay_rows(parts, HEADS_PER_TILE * p + 1), gap,
                                    _decay_rows(parts, HEADS_PER_TILE * p), gap], axis=0)
            dec_ref[0, p, rows, :] = slab.T.astype(BF16)
        _store_head_pairs(k_ref, rows, k, lambda t: _head_norm(t, k_gain, first_half))
        vt = v.T
        vt_ref[0, :, rows] = jnp.concatenate(
            [piece for h in range(N_HEADS) for piece in (vt[h * HEAD_DIM:(h + 1) * HEAD_DIM], ones_rows)],
            axis=0).astype(BF16)
        _store_head_pairs(gate_ref, rows, jax.nn.sigmoid(g))


def _fox_proj(h3d, g, w_in, wf_pad, fb_pad, qg_tile, kg_tile):
    b, s, d = h3d.shape
    row = pl.BlockSpec((1, ROW_TILE, d), lambda i, j: (i, j, 0))
    pairs = pl.BlockSpec((1, d // LANES, ROW_TILE, LANES), lambda i, j: (i, 0, j, 0))
    bf = jax.ShapeDtypeStruct((b, d // LANES, s, LANES), BF16)
    vt_rows = N_HEADS * V_ROWS
    consts = (g, w_in, wf_pad, fb_pad, qg_tile, kg_tile)
    return pl.pallas_call(
        _fox_proj_kernel,
        grid=(b, s // ROW_TILE),
        in_specs=[row] + [_const_spec(w.shape) for w in consts],
        out_specs=[pairs, pairs, pairs, pl.BlockSpec((1, vt_rows, ROW_TILE), lambda i, j: (i, 0, j)), pairs],
        out_shape=[bf, bf, bf, jax.ShapeDtypeStruct((b, vt_rows, s), BF16), bf],
        scratch_shapes=[pltpu.VMEM((1, LANES), F32)],
        compiler_params=_params(("parallel", "arbitrary")),
        name="fox_proj",
    )(h3d, *consts)


def _fox_attn_kernel(q_ref, k_ref, dec_ref, vt_ref, o_ref, *, bounded):
    s_len = q_ref.shape[2]
    lane = lax.broadcasted_iota(jnp.int32, (Q_TILE, LANES), 1).astype(F32).astype(BF16)
    one = jnp.ones((Q_TILE, LANES), BF16)
    visible = (lax.broadcasted_iota(jnp.int32, (Q_TILE, Q_TILE), 0)
               <= lax.broadcasted_iota(jnp.int32, (Q_TILE, Q_TILE), 1))
    nt = (((1,), (1,)), ((), ()))

    def with_decay(x, dec, j, ones_first):
        base = HEAD_DIM * (1 - j)
        start = base if ones_first else base + DECAY_PARTS
        aug = jnp.where(lane >= start, jnp.where(lane < start + DECAY_PARTS, one, dec), dec)
        own = (lane < HEAD_DIM) if j == 0 else (lane >= HEAD_DIM)
        return jnp.where(own, x, aug)

    n_tiles = s_len // Q_TILE
    n_pairs = q_ref.shape[1]
    rows = lambda ref, p, i: ref[0, p, i * Q_TILE:(i + 1) * Q_TILE, :]
    k_heads = {(p, j): jnp.concatenate([with_decay(rows(k_ref, p, i), rows(dec_ref, p, i), j, ones_first=True)
                                        for i in range(n_tiles)], axis=0)
               for p in range(n_pairs) for j in range(HEADS_PER_TILE)}

    def scores(p, i, j):
        qh = with_decay(rows(q_ref, p, i), rows(dec_ref, p, i), j, ones_first=False)
        return lax.dot_general(k_heads[p, j][:(i + 1) * Q_TILE], qh, nt, preferred_element_type=F32)

    def attend(p, i, j, st):
        q0, q1 = i * Q_TILE, (i + 1) * Q_TILE
        diag = jnp.where(visible, st[q0:], -jnp.inf)
        past = st[:q0] if i else None
        if not bounded:
            m = jnp.max(diag, axis=0, keepdims=True)
            if i:
                m = jnp.maximum(m, jnp.max(past, axis=0, keepdims=True))
                past = past - m
            diag = diag - m
        pt = jnp.exp2(diag).astype(BF16)
        if i:
            pt = jnp.concatenate([jnp.exp2(past).astype(BF16), pt], axis=0)
        v0 = (p * HEADS_PER_TILE + j) * V_ROWS
        acc = _dot(vt_ref[0, v0:v0 + V_ROWS, :q1], pt)
        return acc[:HEAD_DIM] * (1.0 / acc[HEAD_DIM:HEAD_DIM + 1])

    units = [(p, i, j) for p in range(n_pairs) for i in range(n_tiles) for j in range(HEADS_PER_TILE)]
    pending = [scores(*u) for u in units[:SCORE_LOOKAHEAD]]
    outs = {}
    for n, (p, i, j) in enumerate(units):
        if n + SCORE_LOOKAHEAD < len(units):
            pending.append(scores(*units[n + SCORE_LOOKAHEAD]))
        outs[j] = attend(p, i, j, pending.pop(0))
        if len(outs) == HEADS_PER_TILE:
            o_t = jnp.concatenate([outs[h] for h in range(HEADS_PER_TILE)], axis=0)
            o_ref[0, p, i * Q_TILE:(i + 1) * Q_TILE, :] = o_t.T.astype(BF16)
            outs = {}


def _fox_attn(q, k, dec, vt, *, bounded):
    b, n_pairs, s, _ = q.shape
    pp = ATTN_PAIRS_PER_STEP_BOUNDED if bounded else ATTN_PAIRS_PER_STEP
    tile = pl.BlockSpec((1, pp, s, LANES), lambda i, j: (i, j, 0, 0))
    return pl.pallas_call(
        functools.partial(_fox_attn_kernel, bounded=bounded),
        grid=(b, n_pairs // pp),
        in_specs=[tile, tile, tile,
                  pl.BlockSpec((1, pp * HEADS_PER_TILE * V_ROWS, s), lambda i, j: (i, j, 0))],
        out_specs=tile,
        out_shape=jax.ShapeDtypeStruct(q.shape, BF16),
        compiler_params=_params(("parallel", "parallel")),
        name="fox_attention_bounded" if bounded else "fox_attention",
    )(q, k, dec, vt)


def _logit_bound(qg, kg):
    return (HEAD_DIM ** 0.5 * LOG2E * 1.02) * jnp.max(jnp.abs(qg)) * jnp.max(jnp.abs(kg))


def kernel(x, norm_mix_g, norm_ffn_g, a_w_in, a_ln_g, a_ln_b, a_w_s, a_b_s, a_w_out,
           b_w_in, b_f_bias, b_q_norm_g, b_k_norm_g, b_w_out,
           ffn_w_gate, ffn_w_up, ffn_w_down):
    bsz, seq, d = x.shape
    row = lambda p: p.reshape(1, -1).astype(F32)
    bf = lambda w: w.astype(BF16)

    b_s_tile = jnp.repeat(a_b_s[0].T, LANES, axis=1)
    h = _gmlp_layer(x.reshape(bsz * seq, d), row(norm_mix_g[0]), bf(a_w_in[0]),
                    row(a_ln_g[0]), row(a_ln_b[0]), a_w_s[0], b_s_tile, bf(a_w_out[0]))
    ffn_w = (bf(ffn_w_gate), bf(ffn_w_up), bf(ffn_w_down))
    h = _ffn_layer(h, 0, row(norm_ffn_g[0]), *ffn_w)

    w_in = bf(b_w_in[0])
    wf_pad = jnp.pad(w_in[:, 4 * d:], ((0, 0), (0, LANES - N_HEADS)))
    fb_pad = jnp.pad(b_f_bias[0], (0, LANES - N_HEADS)).reshape(1, LANES)
    qg_tile = jnp.tile(b_q_norm_g[0], HEADS_PER_TILE).reshape(1, LANES)
    kg_tile = jnp.tile(b_k_norm_g[0], HEADS_PER_TILE).reshape(1, LANES)
    q, k, dec, vt, gate = _fox_proj(h.reshape(bsz, seq, d), row(norm_mix_g[1]),
                                    w_in, wf_pad, fb_pad, qg_tile, kg_tile)
    attn = lax.cond(_logit_bound(b_q_norm_g[0], b_k_norm_g[0]) <= LOGIT_LIMIT,
                    functools.partial(_fox_attn, bounded=True),
                    functools.partial(_fox_attn, bounded=False), q, k, dec, vt)
    h = _ffn_layer(h, 1, row(norm_ffn_g[1]), *ffn_w, attn=attn, gate=gate, w_o=bf(b_w_out[0]))
    return h.reshape(bsz, seq, d)
```

```python
import functools
import math

import jax
import jax.numpy as jnp
from jax import lax
from jax.experimental import pallas as pl
from jax.experimental.pallas import tpu as pltpu

F32 = jnp.float32
BF16 = jnp.bfloat16

EPS = 1e-6
CHUNK = 64
GMLP_BLOCK = 128
GMLP_GROUPS = 8
N_HEADS = 16
HEAD_DIM = 64
LANES = 128
SUBLANES = 8
BF16_SUBLANES = 16
HEADS_PER_TILE = LANES // HEAD_DIM
V_ROWS = HEAD_DIM + BF16_SUBLANES
LOG2E = math.log2(math.e)

ROW_TILE = 1024
PROJ_SUB_TILE = 256
GMLP_ROW_TILE = 1024
GMLP_SUB_TILE = 256
FFN_ROW_TILE = 1024
FFN_SUB_TILE = 256
Q_TILE = 256
SCORE_LOOKAHEAD = 4
ATTN_PAIRS_PER_STEP = 2
ATTN_PAIRS_PER_STEP_BOUNDED = 4
VMEM_LIMIT_BYTES = 56 * 1024 * 1024
LOGIT_LIMIT = 64.0


def _const_spec(shape):
    nd = len(shape)
    return pl.BlockSpec(shape, lambda *_: (0,) * nd, pipeline_mode=pl.Buffered(1))


def _params(semantics):
    return pltpu.CompilerParams(dimension_semantics=semantics,
                                vmem_limit_bytes=VMEM_LIMIT_BYTES)


def _rms_norm(x, g):
    ms = jnp.mean(x * x, axis=-1, keepdims=True)
    return x * lax.rsqrt(ms + EPS) * g


def _dot(a, b):
    return jnp.dot(a, b, preferred_element_type=F32)


def _gelu_exact(x):
    return 0.5 * x * (1.0 + lax.erf(x * math.sqrt(0.5)))


def _split3(x):
    hi = x.astype(BF16).astype(F32)
    r = x - hi
    mid = r.astype(BF16).astype(F32)
    lo = (r - mid).astype(BF16).astype(F32)
    return hi, mid, lo


def _gmlp_kernel(x_ref, g_ref, win_ref, lng_ref, lnb_ref, ws_ref, bs_ref, wout_ref, *rest):
    n_cast = (len(rest) - 1) // 2
    o_ref = rest[n_cast]
    for src_ref, dst_ref in zip(rest[:n_cast], rest[n_cast + 1:]):
        dst_ref[...] = src_ref[...].astype(dst_ref.dtype)

    d = x_ref.shape[-1]
    t_chunk = lax.broadcasted_iota(jnp.int32, (GMLP_BLOCK, GMLP_BLOCK), 0) // CHUNK
    s_chunk = lax.broadcasted_iota(jnp.int32, (GMLP_BLOCK, GMLP_BLOCK), 1) // CHUNK
    visible = s_chunk <= t_chunk
    ws = [jnp.where(visible, ws_ref[g], 0.0).astype(BF16) for g in range(GMLP_GROUPS)]
    bias = bs_ref[...]

    def expand(r):
        x = x_ref[r * GMLP_SUB_TILE:(r + 1) * GMLP_SUB_TILE, :]
        return _dot(_rms_norm(x, g_ref[...]).astype(BF16), win_ref[...])

    def mix(z):
        z = _gelu_exact(z)
        u = z[:, :d]
        v = z[:, d:]
        mu = jnp.mean(v, axis=-1, keepdims=True)
        vc = v - mu
        var = jnp.mean(vc * vc, axis=-1, keepdims=True)
        vn = (vc * lax.rsqrt(var + EPS) * lng_ref[...] + lnb_ref[...]).astype(BF16)
        blocks = []
        for n in range(GMLP_SUB_TILE // GMLP_BLOCK):
            vb = vn[n * GMLP_BLOCK:(n + 1) * GMLP_BLOCK]
            cols = [_dot(ws[g], vb[:, g * LANES:(g + 1) * LANES]) for g in range(GMLP_GROUPS)]
            blocks.append(jnp.concatenate(cols, axis=1) + bias)
        return (u * jnp.concatenate(blocks, axis=0)).astype(BF16)

    def project(r, gated):
        rows = slice(r * GMLP_SUB_TILE, (r + 1) * GMLP_SUB_TILE)
        o_ref[rows, :] = x_ref[rows, :] + _dot(gated, wout_ref[...])

    n_sub = x_ref.shape[0] // GMLP_SUB_TILE
    z = {r: expand(r) for r in range(min(2, n_sub))}
    gated = {0: mix(z.pop(0))}
    for r in range(n_sub):
        if r + 2 < n_sub:
            z[r + 2] = expand(r + 2)
        if r + 1 < n_sub:
            gated[r + 1] = mix(z.pop(r + 1))
        project(r, gated.pop(r))


def _gmlp_layer(x2d, g, w_in, ln_g, ln_b, w_s, b_s_tile, w_out, to_cast):
    t, d = x2d.shape
    steps = t // GMLP_ROW_TILE
    row = pl.BlockSpec((GMLP_ROW_TILE, d), lambda i: (i, 0))
    cast_specs = [pl.BlockSpec((w.shape[0] // steps, w.shape[1]), lambda i: (i, 0)) for w in to_cast]
    return pl.pallas_call(
        _gmlp_kernel,
        grid=(steps,),
        in_specs=[row, _const_spec(g.shape), _const_spec(w_in.shape), _const_spec(ln_g.shape),
                  _const_spec(ln_b.shape), _const_spec(w_s.shape), _const_spec(b_s_tile.shape),
                  _const_spec(w_out.shape)] + cast_specs,
        out_specs=[row] + cast_specs,
        out_shape=[jax.ShapeDtypeStruct((t, d), F32)]
                  + [jax.ShapeDtypeStruct(w.shape, BF16) for w in to_cast],
        compiler_params=_params(("parallel",)),
        name="gmlp_mixer",
    )(x2d, g, w_in, ln_g, ln_b, w_s, b_s_tile, w_out, *to_cast)


def _swiglu_pipelined(residual, g_ref, wg_ref, wu_ref, wd_ref, o_ref):
    def expand(h):
        hn = _rms_norm(h, g_ref[...]).astype(BF16)
        return _dot(hn, wg_ref[...]), _dot(hn, wu_ref[...])

    def contract(r, h, a, b):
        act = (a * jax.nn.sigmoid(a) * b).astype(BF16)
        o_ref[r * FFN_SUB_TILE:(r + 1) * FFN_SUB_TILE, :] = h + _dot(act, wd_ref[...])

    n_sub = o_ref.shape[0] // FFN_SUB_TILE
    h = residual(0)
    a, b = expand(h)
    for r in range(n_sub):
        h_next = residual(r + 1) if r + 1 < n_sub else None
        contract(r, h, a, b)
        if h_next is not None:
            h = h_next
            a, b = expand(h)


def _ffn_kernel(h_ref, g_ref, wg_ref, wu_ref, wd_ref, o_ref):
    residual = lambda r: h_ref[r * FFN_SUB_TILE:(r + 1) * FFN_SUB_TILE, :]
    _swiglu_pipelined(residual, g_ref, wg_ref, wu_ref, wd_ref, o_ref)


def _load_head_pairs(ref, rows):
    return jnp.concatenate([ref[0, p, rows, :] for p in range(ref.shape[1])], axis=1).astype(F32)


def _proj_ffn_kernel(h_ref, a_ref, gate_ref, wo_ref, g_ref, wg_ref, wu_ref, wd_ref, o_ref):
    def residual(r):
        rows = slice(r * FFN_SUB_TILE, (r + 1) * FFN_SUB_TILE)
        a = (_load_head_pairs(a_ref, rows) * _load_head_pairs(gate_ref, rows)).astype(BF16)
        return h_ref[rows, :] + _dot(a, wo_ref[...])
    _swiglu_pipelined(residual, g_ref, wg_ref, wu_ref, wd_ref, o_ref)


def _layer_spec(stacked, layer):
    nd = stacked.ndim - 1
    return pl.BlockSpec((None,) + stacked.shape[1:], lambda *_: (layer,) + (0,) * nd,
                        pipeline_mode=pl.Buffered(1))


def _ffn_layer(h2d, layer, g, w_gate, w_up, w_down, attn=None, gate=None, w_o=None):
    t, d = h2d.shape
    tm = FFN_ROW_TILE
    row = pl.BlockSpec((tm, d), lambda i: (i, 0))
    weights = [g, w_gate, w_up, w_down]
    weight_specs = [_const_spec(g.shape)] + [_layer_spec(w, layer) for w in weights[1:]]
    if attn is None:
        kernel, args, specs, name = _ffn_kernel, [h2d], [row], "swiglu_ffn"
    else:
        tiles_per_seq = attn.shape[2] // tm
        pairs = pl.BlockSpec((1, attn.shape[1], tm, LANES),
                             lambda i: (i // tiles_per_seq, 0, i % tiles_per_seq, 0))
        kernel, args, specs, name = _proj_ffn_kernel, [h2d, attn, gate, w_o], \
            [row, pairs, pairs, _const_spec(w_o.shape)], "attn_out_swiglu_ffn"
    return pl.pallas_call(
        kernel,
        grid=(t // tm,),
        in_specs=specs + weight_specs,
        out_specs=row,
        out_shape=jax.ShapeDtypeStruct((t, d), F32),
        compiler_params=_params(("parallel",)),
        name=name,
    )(*args, *weights)


DECAY_PARTS = 3


def _head_norm(x, gain, first_half):
    sq = x * x
    ss_a = jnp.sum(jnp.where(first_half, sq, 0.0), axis=-1, keepdims=True)
    ss_b = jnp.sum(jnp.where(first_half, 0.0, sq), axis=-1, keepdims=True)
    inv = lax.rsqrt(jnp.where(first_half, ss_a, ss_b) * (1.0 / HEAD_DIM) + EPS)
    return x * inv * gain


def _store_head_pairs(ref, rows, x, fn=None):
    for p in range(ref.shape[1]):
        tile = x[:, p * LANES:(p + 1) * LANES]
        ref[0, p, rows, :] = (tile if fn is None else fn(tile)).astype(ref.dtype)


def _decay_rows(parts, head):
    tm = parts[0].shape[1]
    sub = lax.broadcasted_iota(jnp.int32, (SUBLANES, tm), 0)
    rows = jnp.zeros((SUBLANES, tm), F32)
    for n, part in enumerate(parts):
        r = part[head:head + 1, :]
        rows = jnp.where(sub == n, r, jnp.where(sub == n + DECAY_PARTS, -r, rows))
    return rows


def _fox_proj_kernel(h_ref, g_ref, win_ref, wf_ref, fb_ref, qg_ref, kg_ref,
                     q_ref, k_ref, dec_ref, vt_ref, gate_ref, carry_ref):
    @pl.when(pl.program_id(1) == 0)
    def _():
        carry_ref[...] = jnp.zeros_like(carry_ref)

    d = h_ref.shape[-1]
    tm = PROJ_SUB_TILE
    wq_ref, wk_ref, wv_ref, wg_ref = (win_ref.at[:, n * d:(n + 1) * d] for n in range(4))
    first_half = lax.broadcasted_iota(jnp.int32, (1, LANES), 1) < HEAD_DIM
    q_gain = qg_ref[...] * (HEAD_DIM ** -0.5 * LOG2E)
    k_gain = kg_ref[...]
    lower = (lax.broadcasted_iota(jnp.int32, (tm, tm), 0)
             >= lax.broadcasted_iota(jnp.int32, (tm, tm), 1)).astype(BF16)
    gap = jnp.zeros((HEAD_DIM - SUBLANES, tm), F32)
    ones_rows = jnp.ones((BF16_SUBLANES, tm), F32)

    for r in range(h_ref.shape[1] // tm):
        rows = slice(r * tm, (r + 1) * tm)
        hn = _rms_norm(h_ref[0, rows, :], g_ref[...]).astype(BF16)

        log_f = jax.nn.log_sigmoid(_dot(hn, wf_ref[...]) + fb_ref[...])
        q = _dot(hn, wq_ref[...])
        hi, mid, lo = _split3(log_f)
        cum = carry_ref[...] + (_dot(lower, hi.astype(BF16)) + _dot(lower, mid.astype(BF16))
                                + _dot(lower, lo.astype(BF16)))
        carry_ref[...] = cum[tm - 1:tm, :]
        k = _dot(hn, wk_ref[...])
        v = _dot(hn, wv_ref[...])
        g = _dot(hn, wg_ref[...])

        _store_head_pairs(q_ref, rows, q, lambda t: _head_norm(t, q_gain, first_half))
        parts = _split3(cum.T[:N_HEADS, :] * LOG2E)
        for p in range(dec_ref.shape[1]):
            slab = jnp.concatenate([_decay_rows(parts, HEADS_PER_TILE * p + 1), gap,
                                    _decay_rows(parts, HEADS_PER_TILE * p), gap], axis=0)
            dec_ref[0, p, rows, :] = slab.T.astype(BF16)
        _store_head_pairs(k_ref, rows, k, lambda t: _head_norm(t, k_gain, first_half))
        vt = v.T
        vt_ref[0, :, rows] = jnp.concatenate(
            [piece for h in range(N_HEADS) for piece in (vt[h * HEAD_DIM:(h + 1) * HEAD_DIM], ones_rows)],
            axis=0).astype(BF16)
        _store_head_pairs(gate_ref, rows, jax.nn.sigmoid(g))


def _fox_proj(h3d, g, w_in, wf_pad, fb_pad, qg_tile, kg_tile):
    b, s, d = h3d.shape
    row = pl.BlockSpec((1, ROW_TILE, d), lambda i, j: (i, j, 0))
    pairs = pl.BlockSpec((1, d // LANES, ROW_TILE, LANES), lambda i, j: (i, 0, j, 0))
    bf = jax.ShapeDtypeStruct((b, d // LANES, s, LANES), BF16)
    vt_rows = N_HEADS * V_ROWS
    consts = (g, w_in, wf_pad, fb_pad, qg_tile, kg_tile)
    return pl.pallas_call(
        _fox_proj_kernel,
        grid=(b, s // ROW_TILE),
        in_specs=[row] + [_const_spec(w.shape) for w in consts],
        out_specs=[pairs, pairs, pairs, pl.BlockSpec((1, vt_rows, ROW_TILE), lambda i, j: (i, 0, j)), pairs],
        out_shape=[bf, bf, bf, jax.ShapeDtypeStruct((b, vt_rows, s), BF16), bf],
        scratch_shapes=[pltpu.VMEM((1, LANES), F32)],
        compiler_params=_params(("parallel", "arbitrary")),
        name="fox_proj",
    )(h3d, *consts)


def _fox_attn_kernel(q_ref, k_ref, dec_ref, vt_ref, o_ref, *, bounded):
    s_len = q_ref.shape[2]
    lane = lax.broadcasted_iota(jnp.int32, (Q_TILE, LANES), 1).astype(F32).astype(BF16)
    one = jnp.ones((Q_TILE, LANES), BF16)
    visible = (lax.broadcasted_iota(jnp.int32, (Q_TILE, Q_TILE), 0)
               <= lax.broadcasted_iota(jnp.int32, (Q_TILE, Q_TILE), 1))
    nt = (((1,), (1,)), ((), ()))

    def with_decay(x, dec, j, ones_first):
        base = HEAD_DIM * (1 - j)
        start = base if ones_first else base + DECAY_PARTS
        aug = jnp.where(lane >= start, jnp.where(lane < start + DECAY_PARTS, one, dec), dec)
        own = (lane < HEAD_DIM) if j == 0 else (lane >= HEAD_DIM)
        return jnp.where(own, x, aug)

    n_tiles = s_len // Q_TILE
    n_pairs = q_ref.shape[1]
    rows = lambda ref, p, i: ref[0, p, i * Q_TILE:(i + 1) * Q_TILE, :]
    k_heads = {(p, j): jnp.concatenate([with_decay(rows(k_ref, p, i), rows(dec_ref, p, i), j, ones_first=True)
                                        for i in range(n_tiles)], axis=0)
               for p in range(n_pairs) for j in range(HEADS_PER_TILE)}

    def scores(p, i, j):
        qh = with_decay(rows(q_ref, p, i), rows(dec_ref, p, i), j, ones_first=False)
        return lax.dot_general(k_heads[p, j][:(i + 1) * Q_TILE], qh, nt, preferred_element_type=F32)

    def attend(p, i, j, st):
        q0, q1 = i * Q_TILE, (i + 1) * Q_TILE
        diag = jnp.where(visible, st[q0:], -jnp.inf)
        past = st[:q0] if i else None
        if not bounded:
            m = jnp.max(diag, axis=0, keepdims=True)
            if i:
                m = jnp.maximum(m, jnp.max(past, axis=0, keepdims=True))
                past = past - m
            diag = diag - m
        pt = jnp.exp2(diag).astype(BF16)
        if i:
            pt = jnp.concatenate([jnp.exp2(past).astype(BF16), pt], axis=0)
        v0 = (p * HEADS_PER_TILE + j) * V_ROWS
        acc = _dot(vt_ref[0, v0:v0 + V_ROWS, :q1], pt)
        return acc[:HEAD_DIM] * (1.0 / acc[HEAD_DIM:HEAD_DIM + 1])

    units = [(p, i, j) for p in range(n_pairs) for i in range(n_tiles) for j in range(HEADS_PER_TILE)]
    pending = [scores(*u) for u in units[:SCORE_LOOKAHEAD]]
    outs = {}
    for n, (p, i, j) in enumerate(units):
        if n + SCORE_LOOKAHEAD < len(units):
            pending.append(scores(*units[n + SCORE_LOOKAHEAD]))
        outs[j] = attend(p, i, j, pending.pop(0))
        if len(outs) == HEADS_PER_TILE:
            o_t = jnp.concatenate([outs[h] for h in range(HEADS_PER_TILE)], axis=0)
            o_ref[0, p, i * Q_TILE:(i + 1) * Q_TILE, :] = o_t.T.astype(BF16)
            outs = {}


def _fox_attn(q, k, dec, vt, *, bounded):
    b, n_pairs, s, _ = q.shape
    pp = ATTN_PAIRS_PER_STEP_BOUNDED if bounded else ATTN_PAIRS_PER_STEP
    tile = pl.BlockSpec((1, pp, s, LANES), lambda i, j: (i, j, 0, 0))
    return pl.pallas_call(
        functools.partial(_fox_attn_kernel, bounded=bounded),
        grid=(b, n_pairs // pp),
        in_specs=[tile, tile, tile,
                  pl.BlockSpec((1, pp * HEADS_PER_TILE * V_ROWS, s), lambda i, j: (i, j, 0))],
        out_specs=tile,
        out_shape=jax.ShapeDtypeStruct(q.shape, BF16),
        compiler_params=_params(("parallel", "parallel")),
        name="fox_attention_bounded" if bounded else "fox_attention",
    )(q, k, dec, vt)


def _logit_bound(qg, kg):
    return (HEAD_DIM ** 0.5 * LOG2E * 1.02) * jnp.max(jnp.abs(qg)) * jnp.max(jnp.abs(kg))


def kernel(x, norm_mix_g, norm_ffn_g, a_w_in, a_ln_g, a_ln_b, a_w_s, a_b_s, a_w_out,
           b_w_in, b_f_bias, b_q_norm_g, b_k_norm_g, b_w_out,
           ffn_w_gate, ffn_w_up, ffn_w_down):
    bsz, seq, d = x.shape
    row = lambda p: p.reshape(1, -1).astype(F32)
    bf = lambda w: w.astype(BF16)

    b_s_tile = jnp.repeat(a_b_s[0].T, LANES, axis=1)
    later = (ffn_w_gate, ffn_w_up, ffn_w_down, b_w_in, b_w_out)
    h, *later_bf = _gmlp_layer(x.reshape(bsz * seq, d), row(norm_mix_g[0]), bf(a_w_in[0]),
                               row(a_ln_g[0]), row(a_ln_b[0]), a_w_s[0], b_s_tile, bf(a_w_out[0]),
                               [w.reshape(-1, w.shape[-1]) for w in later])
    *ffn_w, w_in, w_o = (w2.reshape(w.shape) for w2, w in zip(later_bf, later))
    w_in, w_o = w_in[0], w_o[0]
    h = _ffn_layer(h, 0, row(norm_ffn_g[0]), *ffn_w)

    wf_pad = jnp.pad(w_in[:, 4 * d:], ((0, 0), (0, LANES - N_HEADS)))
    fb_pad = jnp.pad(b_f_bias[0], (0, LANES - N_HEADS)).reshape(1, LANES)
    qg_tile = jnp.tile(b_q_norm_g[0], HEADS_PER_TILE).reshape(1, LANES)
    kg_tile = jnp.tile(b_k_norm_g[0], HEADS_PER_TILE).reshape(1, LANES)
    q, k, dec, vt, gate = _fox_proj(h.reshape(bsz, seq, d), row(norm_mix_g[1]),
                                    w_in, wf_pad, fb_pad, qg_tile, kg_tile)
    attn = lax.cond(_logit_bound(b_q_norm_g[0], b_k_norm_g[0]) <= LOGIT_LIMIT,
                    functools.partial(_fox_attn, bounded=True),
                    functools.partial(_fox_attn, bounded=False), q, k, dec, vt)
    h = _ffn_layer(h, 1, row(norm_ffn_g[1]), *ffn_w, attn=attn, gate=gate, w_o=w_o)
    return h.reshape(bsz, seq, d)
```

```python
import functools
import math

import jax
import jax.numpy as jnp
from jax import lax
from jax.experimental import pallas as pl
from jax.experimental.pallas import tpu as pltpu

F32 = jnp.float32
BF16 = jnp.bfloat16

EPS = 1e-6
CHUNK = 64
GMLP_BLOCK = 128
GMLP_GROUPS = 8
N_HEADS = 16
HEAD_DIM = 64
LANES = 128
SUBLANES = 8
BF16_SUBLANES = 16
HEADS_PER_TILE = LANES // HEAD_DIM
V_ROWS = HEAD_DIM + BF16_SUBLANES
LOG2E = math.log2(math.e)

ROW_TILE = 1024
PROJ_SUB_TILE = 256
GMLP_ROW_TILE = 1024
GMLP_SUB_TILE = 256
FFN_ROW_TILE = 1024
FFN_SUB_TILE = 256
Q_TILE = 256
SCORE_LOOKAHEAD = 4
ATTN_PAIRS_PER_STEP = 2
ATTN_PAIRS_PER_STEP_BOUNDED = 4
VMEM_LIMIT_BYTES = 56 * 1024 * 1024
LOGIT_LIMIT = 64.0


def _const_spec(shape):
    nd = len(shape)
    return pl.BlockSpec(shape, lambda *_: (0,) * nd, pipeline_mode=pl.Buffered(1))


def _params(semantics):
    return pltpu.CompilerParams(dimension_semantics=semantics,
                                vmem_limit_bytes=VMEM_LIMIT_BYTES)


def _rms_norm(x, g):
    ms = jnp.mean(x * x, axis=-1, keepdims=True)
    return x * lax.rsqrt(ms + EPS) * g


def _dot(a, b):
    return jnp.dot(a, b, preferred_element_type=F32)


def _gelu_exact(x):
    return 0.5 * x * (1.0 + lax.erf(x * math.sqrt(0.5)))


def _split3(x):
    hi = x.astype(BF16).astype(F32)
    r = x - hi
    mid = r.astype(BF16).astype(F32)
    lo = (r - mid).astype(BF16).astype(F32)
    return hi, mid, lo


def _gmlp_kernel(x_ref, g_ref, win_ref, lng_ref, lnb_ref, ws_ref, bs_ref, wout_ref, *rest):
    n_cast = (len(rest) - 3) // 2
    o_ref = rest[n_cast + 1]
    for src_ref, dst_ref in zip(rest[:n_cast], rest[n_cast + 2:]):
        dst_ref[...] = src_ref[...].astype(dst_ref.dtype)
    rest[-1][...] = rest[n_cast][...].T.astype(rest[-1].dtype)

    d = x_ref.shape[-1]
    t_chunk = lax.broadcasted_iota(jnp.int32, (GMLP_BLOCK, GMLP_BLOCK), 0) // CHUNK
    s_chunk = lax.broadcasted_iota(jnp.int32, (GMLP_BLOCK, GMLP_BLOCK), 1) // CHUNK
    visible = s_chunk <= t_chunk
    ws = [jnp.where(visible, ws_ref[g], 0.0).astype(BF16) for g in range(GMLP_GROUPS)]
    bias = bs_ref[...]

    def expand(r):
        x = x_ref[r * GMLP_SUB_TILE:(r + 1) * GMLP_SUB_TILE, :]
        return _dot(_rms_norm(x, g_ref[...]).astype(BF16), win_ref[...])

    def mix(z):
        z = _gelu_exact(z)
        u = z[:, :d]
        v = z[:, d:]
        mu = jnp.mean(v, axis=-1, keepdims=True)
        vc = v - mu
        var = jnp.mean(vc * vc, axis=-1, keepdims=True)
        vn = (vc * lax.rsqrt(var + EPS) * lng_ref[...] + lnb_ref[...]).astype(BF16)
        blocks = []
        for n in range(GMLP_SUB_TILE // GMLP_BLOCK):
            vb = vn[n * GMLP_BLOCK:(n + 1) * GMLP_BLOCK]
            cols = [_dot(ws[g], vb[:, g * LANES:(g + 1) * LANES]) for g in range(GMLP_GROUPS)]
            blocks.append(jnp.concatenate(cols, axis=1) + bias)
        return (u * jnp.concatenate(blocks, axis=0)).astype(BF16)

    def project(r, gated):
        rows = slice(r * GMLP_SUB_TILE, (r + 1) * GMLP_SUB_TILE)
        o_ref[rows, :] = x_ref[rows, :] + _dot(gated, wout_ref[...])

    n_sub = x_ref.shape[0] // GMLP_SUB_TILE
    z = {r: expand(r) for r in range(min(2, n_sub))}
    gated = {0: mix(z.pop(0))}
    for r in range(n_sub):
        if r + 2 < n_sub:
            z[r + 2] = expand(r + 2)
        if r + 1 < n_sub:
            gated[r + 1] = mix(z.pop(r + 1))
        project(r, gated.pop(r))


def _gmlp_layer(x2d, g, w_in, ln_g, ln_b, w_s, b_s_tile, w_out, to_cast, wt, wt_rows):
    t, d = x2d.shape
    steps = t // GMLP_ROW_TILE
    row = pl.BlockSpec((GMLP_ROW_TILE, d), lambda i: (i, 0))
    cast_specs = [pl.BlockSpec((w.shape[0] // steps, w.shape[1]), lambda i: (i, 0)) for w in to_cast]
    blk = wt_rows // steps
    return pl.pallas_call(
        _gmlp_kernel,
        grid=(steps,),
        in_specs=[row, _const_spec(g.shape), _const_spec(w_in.shape), _const_spec(ln_g.shape),
                  _const_spec(ln_b.shape), _const_spec(w_s.shape), _const_spec(b_s_tile.shape),
                  _const_spec(w_out.shape)] + cast_specs
                 + [pl.BlockSpec((blk, wt.shape[1]), lambda i: (i, 0))],
        out_specs=[row] + cast_specs + [pl.BlockSpec((wt.shape[1], blk), lambda i: (0, i))],
        out_shape=[jax.ShapeDtypeStruct((t, d), F32)]
                  + [jax.ShapeDtypeStruct(w.shape, BF16) for w in to_cast]
                  + [jax.ShapeDtypeStruct((wt.shape[1], wt_rows), BF16)],
        compiler_params=_params(("parallel",)),
        name="gmlp_mixer",
    )(x2d, g, w_in, ln_g, ln_b, w_s, b_s_tile, w_out, *to_cast, wt)


def _swiglu_pipelined(residual, g_ref, wg_ref, wu_ref, wd_ref, o_ref):
    def expand(h):
        hn = _rms_norm(h, g_ref[...]).astype(BF16)
        return _dot(hn, wg_ref[...]), _dot(hn, wu_ref[...])

    def contract(r, h, a, b):
        act = (a * jax.nn.sigmoid(a) * b).astype(BF16)
        o_ref[r * FFN_SUB_TILE:(r + 1) * FFN_SUB_TILE, :] = h + _dot(act, wd_ref[...])

    n_sub = o_ref.shape[0] // FFN_SUB_TILE
    h = residual(0)
    a, b = expand(h)
    for r in range(n_sub):
        h_next = residual(r + 1) if r + 1 < n_sub else None
        contract(r, h, a, b)
        if h_next is not None:
            h = h_next
            a, b = expand(h)


def _ffn_kernel(h_ref, g_ref, wg_ref, wu_ref, wd_ref, o_ref):
    residual = lambda r: h_ref[r * FFN_SUB_TILE:(r + 1) * FFN_SUB_TILE, :]
    _swiglu_pipelined(residual, g_ref, wg_ref, wu_ref, wd_ref, o_ref)


def _load_head_pairs(ref, rows):
    return jnp.concatenate([ref[0, p, rows, :] for p in range(ref.shape[1])], axis=1).astype(F32)


def _proj_ffn_kernel(h_ref, a_ref, gate_ref, wo_ref, g_ref, wg_ref, wu_ref, wd_ref, o_ref):
    def residual(r):
        rows = slice(r * FFN_SUB_TILE, (r + 1) * FFN_SUB_TILE)
        a = (_load_head_pairs(a_ref, rows) * _load_head_pairs(gate_ref, rows)).astype(BF16)
        return h_ref[rows, :] + _dot(a, wo_ref[...])
    _swiglu_pipelined(residual, g_ref, wg_ref, wu_ref, wd_ref, o_ref)


def _layer_spec(stacked, layer):
    nd = stacked.ndim - 1
    return pl.BlockSpec((None,) + stacked.shape[1:], lambda *_: (layer,) + (0,) * nd,
                        pipeline_mode=pl.Buffered(1))


def _ffn_layer(h2d, layer, g, w_gate, w_up, w_down, attn=None, gate=None, w_o=None):
    t, d = h2d.shape
    tm = FFN_ROW_TILE
    row = pl.BlockSpec((tm, d), lambda i: (i, 0))
    weights = [g, w_gate, w_up, w_down]
    weight_specs = [_const_spec(g.shape)] + [_layer_spec(w, layer) for w in weights[1:]]
    if attn is None:
        kernel, args, specs, name = _ffn_kernel, [h2d], [row], "swiglu_ffn"
    else:
        tiles_per_seq = attn.shape[2] // tm
        pairs = pl.BlockSpec((1, attn.shape[1], tm, LANES),
                             lambda i: (i // tiles_per_seq, 0, i % tiles_per_seq, 0))
        kernel, args, specs, name = _proj_ffn_kernel, [h2d, attn, gate, w_o], \
            [row, pairs, pairs, _const_spec(w_o.shape)], "attn_out_swiglu_ffn"
    return pl.pallas_call(
        kernel,
        grid=(t // tm,),
        in_specs=specs + weight_specs,
        out_specs=row,
        out_shape=jax.ShapeDtypeStruct((t, d), F32),
        compiler_params=_params(("parallel",)),
        name=name,
    )(*args, *weights)


DECAY_PARTS = 3


def _head_norm(x, gain, first_half):
    sq = x * x
    ss_a = jnp.sum(jnp.where(first_half, sq, 0.0), axis=-1, keepdims=True)
    ss_b = jnp.sum(jnp.where(first_half, 0.0, sq), axis=-1, keepdims=True)
    inv = lax.rsqrt(jnp.where(first_half, ss_a, ss_b) * (1.0 / HEAD_DIM) + EPS)
    return x * inv * gain


def _store_head_pairs(ref, rows, x, fn=None):
    for p in range(ref.shape[1]):
        tile = x[:, p * LANES:(p + 1) * LANES]
        ref[0, p, rows, :] = (tile if fn is None else fn(tile)).astype(ref.dtype)


def _decay_rows(parts, head):
    tm = parts[0].shape[1]
    sub = lax.broadcasted_iota(jnp.int32, (SUBLANES, tm), 0)
    rows = jnp.zeros((SUBLANES, tm), F32)
    for n, part in enumerate(parts):
        r = part[head:head + 1, :]
        rows = jnp.where(sub == n, r, jnp.where(sub == n + DECAY_PARTS, -r, rows))
    return rows


def _fox_proj_kernel(h_ref, g_ref, win_ref, wf_ref, fb_ref, qg_ref, kg_ref,
                     q_ref, k_ref, dec_ref, vt_ref, gate_ref, carry_ref):
    @pl.when(pl.program_id(1) == 0)
    def _():
        carry_ref[...] = jnp.zeros_like(carry_ref)

    d = h_ref.shape[-1]
    tm = PROJ_SUB_TILE
    wq_ref, wk_ref, wv_ref, wg_ref = (win_ref.at[:, n * d:(n + 1) * d] for n in range(4))
    first_half = lax.broadcasted_iota(jnp.int32, (1, LANES), 1) < HEAD_DIM
    q_gain = qg_ref[...] * (HEAD_DIM ** -0.5 * LOG2E)
    k_gain = kg_ref[...]
    lower = (lax.broadcasted_iota(jnp.int32, (tm, tm), 0)
             >= lax.broadcasted_iota(jnp.int32, (tm, tm), 1)).astype(BF16)
    gap = jnp.zeros((HEAD_DIM - SUBLANES, tm), F32)
    ones_rows = jnp.ones((BF16_SUBLANES, tm), F32)
    wf_t = jnp.concatenate([wf_ref[...], jnp.zeros((LANES - N_HEADS, d), F32)], axis=0).astype(BF16)
    nt = (((1,), (1,)), ((), ()))

    for r in range(h_ref.shape[1] // tm):
        rows = slice(r * tm, (r + 1) * tm)
        hn = _rms_norm(h_ref[0, rows, :], g_ref[...]).astype(BF16)

        f_logit = lax.dot_general(hn, wf_t, nt, preferred_element_type=F32)
        log_f = jax.nn.log_sigmoid(f_logit + fb_ref[...])
        q = _dot(hn, wq_ref[...])
        hi, mid, lo = _split3(log_f)
        cum = carry_ref[...] + (_dot(lower, hi.astype(BF16)) + _dot(lower, mid.astype(BF16))
                                + _dot(lower, lo.astype(BF16)))
        carry_ref[...] = cum[tm - 1:tm, :]
        k = _dot(hn, wk_ref[...])
        v = _dot(hn, wv_ref[...])
        g = _dot(hn, wg_ref[...])

        _store_head_pairs(q_ref, rows, q, lambda t: _head_norm(t, q_gain, first_half))
        parts = _split3(cum.T[:N_HEADS, :] * LOG2E)
        for p in range(dec_ref.shape[1]):
            slab = jnp.concatenate([_decay_rows(parts, HEADS_PER_TILE * p + 1), gap,
                                    _decay_rows(parts, HEADS_PER_TILE * p), gap], axis=0)
            dec_ref[0, p, rows, :] = slab.T.astype(BF16)
        _store_head_pairs(k_ref, rows, k, lambda t: _head_norm(t, k_gain, first_half))
        vt = v.T
        vt_ref[0, :, rows] = jnp.concatenate(
            [piece for h in range(N_HEADS) for piece in (vt[h * HEAD_DIM:(h + 1) * HEAD_DIM], ones_rows)],
            axis=0).astype(BF16)
        _store_head_pairs(gate_ref, rows, jax.nn.sigmoid(g))


def _fox_proj(h3d, g, w_in, w_in_t, fb_pad, qg_tile, kg_tile):
    b, s, d = h3d.shape
    row = pl.BlockSpec((1, ROW_TILE, d), lambda i, j: (i, j, 0))
    pairs = pl.BlockSpec((1, d // LANES, ROW_TILE, LANES), lambda i, j: (i, 0, j, 0))
    bf = jax.ShapeDtypeStruct((b, d // LANES, s, LANES), BF16)
    vt_rows = N_HEADS * V_ROWS
    f_block = (w_in_t.shape[0] - N_HEADS) // N_HEADS
    wf_spec = pl.BlockSpec((N_HEADS, d), lambda *_: (f_block, 0), pipeline_mode=pl.Buffered(1))
    consts = (g, w_in, w_in_t, fb_pad, qg_tile, kg_tile)
    return pl.pallas_call(
        _fox_proj_kernel,
        grid=(b, s // ROW_TILE),
        in_specs=[row] + [wf_spec if w is w_in_t else _const_spec(w.shape) for w in consts],
        out_specs=[pairs, pairs, pairs, pl.BlockSpec((1, vt_rows, ROW_TILE), lambda i, j: (i, 0, j)), pairs],
        out_shape=[bf, bf, bf, jax.ShapeDtypeStruct((b, vt_rows, s), BF16), bf],
        scratch_shapes=[pltpu.VMEM((1, LANES), F32)],
        compiler_params=_params(("parallel", "arbitrary")),
        name="fox_proj",
    )(h3d, *consts)


def _fox_attn_kernel(q_ref, k_ref, dec_ref, vt_ref, o_ref, *, bounded):
    s_len = q_ref.shape[2]
    lane = lax.broadcasted_iota(jnp.int32, (Q_TILE, LANES), 1).astype(F32).astype(BF16)
    one = jnp.ones((Q_TILE, LANES), BF16)
    visible = (lax.broadcasted_iota(jnp.int32, (Q_TILE, Q_TILE), 0)
               <= lax.broadcasted_iota(jnp.int32, (Q_TILE, Q_TILE), 1))
    nt = (((1,), (1,)), ((), ()))

    def with_decay(x, dec, j, ones_first):
        base = HEAD_DIM * (1 - j)
        start = base if ones_first else base + DECAY_PARTS
        aug = jnp.where(lane >= start, jnp.where(lane < start + DECAY_PARTS, one, dec), dec)
        own = (lane < HEAD_DIM) if j == 0 else (lane >= HEAD_DIM)
        return jnp.where(own, x, aug)

    n_tiles = s_len // Q_TILE
    n_pairs = q_ref.shape[1]
    rows = lambda ref, p, i: ref[0, p, i * Q_TILE:(i + 1) * Q_TILE, :]
    k_heads = {(p, j): jnp.concatenate([with_decay(rows(k_ref, p, i), rows(dec_ref, p, i), j, ones_first=True)
                                        for i in range(n_tiles)], axis=0)
               for p in range(n_pairs) for j in range(HEADS_PER_TILE)}

    def scores(p, i, j):
        qh = with_decay(rows(q_ref, p, i), rows(dec_ref, p, i), j, ones_first=False)
        return lax.dot_general(k_heads[p, j][:(i + 1) * Q_TILE], qh, nt, preferred_element_type=F32)

    def attend(p, i, j, st):
        q0, q1 = i * Q_TILE, (i + 1) * Q_TILE
        diag = jnp.where(visible, st[q0:], -jnp.inf)
        past = st[:q0] if i else None
        if not bounded:
            m = jnp.max(diag, axis=0, keepdims=True)
            if i:
                m = jnp.maximum(m, jnp.max(past, axis=0, keepdims=True))
                past = past - m
            diag = diag - m
        pt = jnp.exp2(diag).astype(BF16)
        if i:
            pt = jnp.concatenate([jnp.exp2(past).astype(BF16), pt], axis=0)
        v0 = (p * HEADS_PER_TILE + j) * V_ROWS
        acc = _dot(vt_ref[0, v0:v0 + V_ROWS, :q1], pt)
        return acc[:HEAD_DIM] * (1.0 / acc[HEAD_DIM:HEAD_DIM + 1])

    units = [(p, i, j) for p in range(n_pairs) for i in range(n_tiles) for j in range(HEADS_PER_TILE)]
    pending = [scores(*u) for u in units[:SCORE_LOOKAHEAD]]
    outs = {}
    for n, (p, i, j) in enumerate(units):
        if n + SCORE_LOOKAHEAD < len(units):
            pending.append(scores(*units[n + SCORE_LOOKAHEAD]))
        outs[j] = attend(p, i, j, pending.pop(0))
        if len(outs) == HEADS_PER_TILE:
            o_t = jnp.concatenate([outs[h] for h in range(HEADS_PER_TILE)], axis=0)
            o_ref[0, p, i * Q_TILE:(i + 1) * Q_TILE, :] = o_t.T.astype(BF16)
            outs = {}


def _fox_attn(q, k, dec, vt, *, bounded):
    b, n_pairs, s, _ = q.shape
    pp = ATTN_PAIRS_PER_STEP_BOUNDED if bounded else ATTN_PAIRS_PER_STEP
    tile = pl.BlockSpec((1, pp, s, LANES), lambda i, j: (i, j, 0, 0))
    return pl.pallas_call(
        functools.partial(_fox_attn_kernel, bounded=bounded),
        grid=(b, n_pairs // pp),
        in_specs=[tile, tile, tile,
                  pl.BlockSpec((1, pp * HEADS_PER_TILE * V_ROWS, s), lambda i, j: (i, j, 0))],
        out_specs=tile,
        out_shape=jax.ShapeDtypeStruct(q.shape, BF16),
        compiler_params=_params(("parallel", "parallel")),
        name="fox_attention_bounded" if bounded else "fox_attention",
    )(q, k, dec, vt)


def _logit_bound(qg, kg):
    return (HEAD_DIM ** 0.5 * LOG2E * 1.02) * jnp.max(jnp.abs(qg)) * jnp.max(jnp.abs(kg))


def kernel(x, norm_mix_g, norm_ffn_g, a_w_in, a_ln_g, a_ln_b, a_w_s, a_b_s, a_w_out,
           b_w_in, b_f_bias, b_q_norm_g, b_k_norm_g, b_w_out,
           ffn_w_gate, ffn_w_up, ffn_w_down):
    bsz, seq, d = x.shape
    row = lambda p: p.reshape(1, -1).astype(F32)
    bf = lambda w: w.astype(BF16)

    b_s_tile = jnp.repeat(a_b_s[0].T, LANES, axis=1)
    later = (ffn_w_gate, ffn_w_up, ffn_w_down, b_w_out)
    w_in_t = jnp.swapaxes(b_w_in[0], 0, 1)
    h, *later_bf, w_in = _gmlp_layer(x.reshape(bsz * seq, d), row(norm_mix_g[0]), bf(a_w_in[0]),
                                     row(a_ln_g[0]), row(a_ln_b[0]), a_w_s[0], b_s_tile, bf(a_w_out[0]),
                                     [w.reshape(-1, w.shape[-1]) for w in later], w_in_t, 4 * d)
    *ffn_w, w_o = (w2.reshape(w.shape) for w2, w in zip(later_bf, later))
    h = _ffn_layer(h, 0, row(norm_ffn_g[0]), *ffn_w)

    fb_pad = jnp.pad(b_f_bias[0], (0, LANES - N_HEADS)).reshape(1, LANES)
    qg_tile = jnp.tile(b_q_norm_g[0], HEADS_PER_TILE).reshape(1, LANES)
    kg_tile = jnp.tile(b_k_norm_g[0], HEADS_PER_TILE).reshape(1, LANES)
    q, k, dec, vt, gate = _fox_proj(h.reshape(bsz, seq, d), row(norm_mix_g[1]),
                                    w_in, w_in_t, fb_pad, qg_tile, kg_tile)
    attn = lax.cond(_logit_bound(b_q_norm_g[0], b_k_norm_g[0]) <= LOGIT_LIMIT,
                    functools.partial(_fox_attn, bounded=True),
                    functools.partial(_fox_attn, bounded=False), q, k, dec, vt)
    h = _ffn_layer(h, 1, row(norm_ffn_g[1]), *ffn_w, attn=attn, gate=gate, w_o=w_o[0])
    return h.reshape(bsz, seq, d)
```

```python
import functools
import math

import jax
import jax.numpy as jnp
from jax import lax
from jax.experimental import pallas as pl
from jax.experimental.pallas import tpu as pltpu

F32 = jnp.float32
BF16 = jnp.bfloat16

EPS = 1e-6
CHUNK = 64
GMLP_BLOCK = 128
GMLP_GROUPS = 8
N_HEADS = 16
HEAD_DIM = 64
LANES = 128
SUBLANES = 8
BF16_SUBLANES = 16
HEADS_PER_TILE = LANES // HEAD_DIM
V_ROWS = HEAD_DIM + BF16_SUBLANES
LOG2E = math.log2(math.e)

ROW_TILE = 1024
PROJ_SUB_TILE = 256
GMLP_ROW_TILE = 1024
GMLP_SUB_TILE = 256
FFN_ROW_TILE = 1024
FFN_SUB_TILE = 256
Q_TILE = 256
SCORE_LOOKAHEAD = 4
ATTN_PAIRS_PER_STEP = 2
ATTN_PAIRS_PER_STEP_BOUNDED = 4
VMEM_LIMIT_BYTES = 56 * 1024 * 1024
LOGIT_LIMIT = 64.0


def _const_spec(shape):
    nd = len(shape)
    return pl.BlockSpec(shape, lambda *_: (0,) * nd, pipeline_mode=pl.Buffered(1))


def _params(semantics):
    return pltpu.CompilerParams(dimension_semantics=semantics,
                                vmem_limit_bytes=VMEM_LIMIT_BYTES)


def _rms_norm(x, g):
    ms = jnp.mean(x * x, axis=-1, keepdims=True)
    return x * lax.rsqrt(ms + EPS) * g


def _dot(a, b):
    return jnp.dot(a, b, preferred_element_type=F32)


def _gelu_exact(x):
    return 0.5 * x * (1.0 + lax.erf(x * math.sqrt(0.5)))


def _split3(x):
    hi = x.astype(BF16).astype(F32)
    r = x - hi
    mid = r.astype(BF16).astype(F32)
    lo = (r - mid).astype(BF16).astype(F32)
    return hi, mid, lo


def _gmlp_kernel(x_ref, g_ref, win_ref, lng_ref, lnb_ref, ws_ref, bs_ref, wout_ref, *rest):
    n_cast = (len(rest) - 3) // 2
    o_ref = rest[n_cast + 1]
    for src_ref, dst_ref in zip(rest[:n_cast], rest[n_cast + 2:]):
        dst_ref[...] = src_ref[...].astype(dst_ref.dtype)
    rest[-1][...] = rest[n_cast][...].T.astype(rest[-1].dtype)

    d = x_ref.shape[-1]
    t_chunk = lax.broadcasted_iota(jnp.int32, (GMLP_BLOCK, GMLP_BLOCK), 0) // CHUNK
    s_chunk = lax.broadcasted_iota(jnp.int32, (GMLP_BLOCK, GMLP_BLOCK), 1) // CHUNK
    visible = s_chunk <= t_chunk
    ws = [jnp.where(visible, ws_ref[g], 0.0).astype(BF16) for g in range(GMLP_GROUPS)]
    bias = bs_ref[...]

    def expand(r):
        x = x_ref[r * GMLP_SUB_TILE:(r + 1) * GMLP_SUB_TILE, :]
        return _dot(_rms_norm(x, g_ref[...]).astype(BF16), win_ref[...])

    def mix(z):
        z = _gelu_exact(z)
        u = z[:, :d]
        v = z[:, d:]
        mu = jnp.mean(v, axis=-1, keepdims=True)
        vc = v - mu
        var = jnp.mean(vc * vc, axis=-1, keepdims=True)
        vn = (vc * lax.rsqrt(var + EPS) * lng_ref[...] + lnb_ref[...]).astype(BF16)
        blocks = []
        for n in range(GMLP_SUB_TILE // GMLP_BLOCK):
            vb = vn[n * GMLP_BLOCK:(n + 1) * GMLP_BLOCK]
            cols = [_dot(ws[g], vb[:, g * LANES:(g + 1) * LANES]) for g in range(GMLP_GROUPS)]
            blocks.append(jnp.concatenate(cols, axis=1) + bias)
        return (u * jnp.concatenate(blocks, axis=0)).astype(BF16)

    def project(r, gated):
        rows = slice(r * GMLP_SUB_TILE, (r + 1) * GMLP_SUB_TILE)
        o_ref[rows, :] = x_ref[rows, :] + _dot(gated, wout_ref[...])

    n_sub = x_ref.shape[0] // GMLP_SUB_TILE
    z = {r: expand(r) for r in range(min(2, n_sub))}
    gated = {0: mix(z.pop(0))}
    for r in range(n_sub):
        if r + 2 < n_sub:
            z[r + 2] = expand(r + 2)
        if r + 1 < n_sub:
            gated[r + 1] = mix(z.pop(r + 1))
        project(r, gated.pop(r))


def _gmlp_layer(x2d, g, w_in, ln_g, ln_b, w_s, b_s_tile, w_out, to_cast, wt, wt_rows):
    t, d = x2d.shape
    steps = t // GMLP_ROW_TILE
    row = pl.BlockSpec((GMLP_ROW_TILE, d), lambda i: (i, 0))
    cast_specs = [pl.BlockSpec((w.shape[0] // steps, w.shape[1]), lambda i: (i, 0)) for w in to_cast]
    blk = wt_rows // steps
    return pl.pallas_call(
        _gmlp_kernel,
        grid=(steps,),
        in_specs=[row, _const_spec(g.shape), _const_spec(w_in.shape), _const_spec(ln_g.shape),
                  _const_spec(ln_b.shape), _const_spec(w_s.shape), _const_spec(b_s_tile.shape),
                  _const_spec(w_out.shape)] + cast_specs
                 + [pl.BlockSpec((blk, wt.shape[1]), lambda i: (i, 0))],
        out_specs=[row] + cast_specs + [pl.BlockSpec((wt.shape[1], blk), lambda i: (0, i))],
        out_shape=[jax.ShapeDtypeStruct((t, d), F32)]
                  + [jax.ShapeDtypeStruct(w.shape, BF16) for w in to_cast]
                  + [jax.ShapeDtypeStruct((wt.shape[1], wt_rows), BF16)],
        compiler_params=_params(("parallel",)),
        name="gmlp_mixer",
    )(x2d, g, w_in, ln_g, ln_b, w_s, b_s_tile, w_out, *to_cast, wt)


def _swiglu_pipelined(residual, g_ref, wg_ref, wu_ref, wd_ref, o_ref):
    def expand(h):
        hn = _rms_norm(h, g_ref[...]).astype(BF16)
        return _dot(hn, wg_ref[...]), _dot(hn, wu_ref[...])

    def contract(r, h, a, b):
        act = (a * jax.nn.sigmoid(a) * b).astype(BF16)
        o_ref[r * FFN_SUB_TILE:(r + 1) * FFN_SUB_TILE, :] = h + _dot(act, wd_ref[...])

    n_sub = o_ref.shape[0] // FFN_SUB_TILE
    h = residual(0)
    a, b = expand(h)
    for r in range(n_sub):
        h_next = residual(r + 1) if r + 1 < n_sub else None
        contract(r, h, a, b)
        if h_next is not None:
            h = h_next
            a, b = expand(h)


def _ffn_kernel(h_ref, g_ref, wg_ref, wu_ref, wd_ref, o_ref):
    residual = lambda r: h_ref[r * FFN_SUB_TILE:(r + 1) * FFN_SUB_TILE, :]
    _swiglu_pipelined(residual, g_ref, wg_ref, wu_ref, wd_ref, o_ref)


def _load_head_pairs(ref, rows):
    return jnp.concatenate([ref[0, p, rows, :] for p in range(ref.shape[1])], axis=1).astype(F32)


def _proj_ffn_kernel(h_ref, a_ref, gate_ref, wo_ref, g_ref, wg_ref, wu_ref, wd_ref, o_ref):
    def residual(r):
        rows = slice(r * FFN_SUB_TILE, (r + 1) * FFN_SUB_TILE)
        a = (_load_head_pairs(a_ref, rows) * _load_head_pairs(gate_ref, rows)).astype(BF16)
        return h_ref[rows, :] + _dot(a, wo_ref[...])
    _swiglu_pipelined(residual, g_ref, wg_ref, wu_ref, wd_ref, o_ref)


def _layer_spec(stacked, layer):
    nd = stacked.ndim - 1
    return pl.BlockSpec((None,) + stacked.shape[1:], lambda *_: (layer,) + (0,) * nd,
                        pipeline_mode=pl.Buffered(1))


def _ffn_layer(h2d, layer, g, w_gate, w_up, w_down, attn=None, gate=None, w_o=None):
    t, d = h2d.shape
    tm = FFN_ROW_TILE
    row = pl.BlockSpec((tm, d), lambda i: (i, 0))
    weights = [g, w_gate, w_up, w_down]
    weight_specs = [_const_spec(g.shape)] + [_layer_spec(w, layer) for w in weights[1:]]
    if attn is None:
        kernel, args, specs, name = _ffn_kernel, [h2d], [row], "swiglu_ffn"
    else:
        tiles_per_seq = attn.shape[2] // tm
        pairs = pl.BlockSpec((1, attn.shape[1], tm, LANES),
                             lambda i: (i // tiles_per_seq, 0, i % tiles_per_seq, 0))
        kernel, args, specs, name = _proj_ffn_kernel, [h2d, attn, gate, w_o], \
            [row, pairs, pairs, _const_spec(w_o.shape)], "attn_out_swiglu_ffn"
    return pl.pallas_call(
        kernel,
        grid=(t // tm,),
        in_specs=specs + weight_specs,
        out_specs=row,
        out_shape=jax.ShapeDtypeStruct((t, d), F32),
        compiler_params=_params(("parallel",)),
        name=name,
    )(*args, *weights)


DECAY_PARTS = 3


def _head_norm(x, gain, first_half):
    sq = x * x
    ss_a = jnp.sum(jnp.where(first_half, sq, 0.0), axis=-1, keepdims=True)
    ss_b = jnp.sum(jnp.where(first_half, 0.0, sq), axis=-1, keepdims=True)
    inv = lax.rsqrt(jnp.where(first_half, ss_a, ss_b) * (1.0 / HEAD_DIM) + EPS)
    return x * inv * gain


def _store_head_pairs(ref, rows, x, fn=None):
    for p in range(ref.shape[1]):
        tile = x[:, p * LANES:(p + 1) * LANES]
        ref[0, p, rows, :] = (tile if fn is None else fn(tile)).astype(ref.dtype)


def _decay_rows(parts, head):
    tm = parts[0].shape[1]
    sub = lax.broadcasted_iota(jnp.int32, (SUBLANES, tm), 0)
    rows = jnp.zeros((SUBLANES, tm), F32)
    for n, part in enumerate(parts):
        r = part[head:head + 1, :]
        rows = jnp.where(sub == n, r, jnp.where(sub == n + DECAY_PARTS, -r, rows))
    return rows


def _fox_proj_kernel(h_ref, g_ref, win_ref, wf_ref, fb_ref, qg_ref, kg_ref,
                     q_ref, k_ref, dec_ref, vt_ref, gate_ref, carry_ref):
    @pl.when(pl.program_id(1) == 0)
    def _():
        carry_ref[...] = jnp.zeros_like(carry_ref)

    d = h_ref.shape[-1]
    tm = PROJ_SUB_TILE
    wq_ref, wk_ref, wv_ref, wg_ref = (win_ref.at[:, n * d:(n + 1) * d] for n in range(4))
    first_half = lax.broadcasted_iota(jnp.int32, (1, LANES), 1) < HEAD_DIM
    q_gain = qg_ref[...] * (HEAD_DIM ** -0.5 * LOG2E)
    k_gain = kg_ref[...]
    pos = lax.broadcasted_iota(jnp.int32, (N_HEADS, tm), 1)
    gap = jnp.zeros((HEAD_DIM - SUBLANES, tm), F32)
    ones_rows = jnp.ones((BF16_SUBLANES, tm), F32)
    wf_t = jnp.concatenate([wf_ref[...], jnp.zeros((LANES - N_HEADS, d), F32)], axis=0).astype(BF16)
    nt = (((1,), (1,)), ((), ()))

    for r in range(h_ref.shape[1] // tm):
        rows = slice(r * tm, (r + 1) * tm)
        hn = _rms_norm(h_ref[0, rows, :], g_ref[...]).astype(BF16)

        f_logit = lax.dot_general(hn, wf_t, nt, preferred_element_type=F32)
        log_f = jax.nn.log_sigmoid(f_logit + fb_ref[...])
        q = _dot(hn, wq_ref[...])
        k = _dot(hn, wk_ref[...])
        v = _dot(hn, wv_ref[...])
        g = _dot(hn, wg_ref[...])

        cum = log_f.T[:N_HEADS, :]
        shift = 1
        while shift < tm:
            cum = cum + jnp.where(pos >= shift, pltpu.roll(cum, shift, axis=1), 0.0)
            shift *= 2
        cum = cum + carry_ref[...]
        carry_ref[...] = cum[:, tm - 1:tm]

        _store_head_pairs(q_ref, rows, q, lambda t: _head_norm(t, q_gain, first_half))
        parts = _split3(cum * LOG2E)
        for p in range(dec_ref.shape[1]):
            slab = jnp.concatenate([_decay_rows(parts, HEADS_PER_TILE * p + 1), gap,
                                    _decay_rows(parts, HEADS_PER_TILE * p), gap], axis=0)
            dec_ref[0, p, rows, :] = slab.T.astype(BF16)
        _store_head_pairs(k_ref, rows, k, lambda t: _head_norm(t, k_gain, first_half))
        vt = v.T
        vt_ref[0, :, rows] = jnp.concatenate(
            [piece for h in range(N_HEADS) for piece in (vt[h * HEAD_DIM:(h + 1) * HEAD_DIM], ones_rows)],
            axis=0).astype(BF16)
        _store_head_pairs(gate_ref, rows, jax.nn.sigmoid(g))


def _fox_proj(h3d, g, w_in, w_in_t, fb_pad, qg_tile, kg_tile):
    b, s, d = h3d.shape
    row = pl.BlockSpec((1, ROW_TILE, d), lambda i, j: (i, j, 0))
    pairs = pl.BlockSpec((1, d // LANES, ROW_TILE, LANES), lambda i, j: (i, 0, j, 0))
    bf = jax.ShapeDtypeStruct((b, d // LANES, s, LANES), BF16)
    vt_rows = N_HEADS * V_ROWS
    f_block = (w_in_t.shape[0] - N_HEADS) // N_HEADS
    wf_spec = pl.BlockSpec((N_HEADS, d), lambda *_: (f_block, 0), pipeline_mode=pl.Buffered(1))
    consts = (g, w_in, w_in_t, fb_pad, qg_tile, kg_tile)
    return pl.pallas_call(
        _fox_proj_kernel,
        grid=(b, s // ROW_TILE),
        in_specs=[row] + [wf_spec if w is w_in_t else _const_spec(w.shape) for w in consts],
        out_specs=[pairs, pairs, pairs, pl.BlockSpec((1, vt_rows, ROW_TILE), lambda i, j: (i, 0, j)), pairs],
        out_shape=[bf, bf, bf, jax.ShapeDtypeStruct((b, vt_rows, s), BF16), bf],
        scratch_shapes=[pltpu.VMEM((N_HEADS, 1), F32)],
        compiler_params=_params(("parallel", "arbitrary")),
        name="fox_proj",
    )(h3d, *consts)


def _fox_attn_kernel(q_ref, k_ref, dec_ref, vt_ref, o_ref, *, bounded):
    s_len = q_ref.shape[2]
    lane = lax.broadcasted_iota(jnp.int32, (Q_TILE, LANES), 1).astype(F32).astype(BF16)
    one = jnp.ones((Q_TILE, LANES), BF16)
    visible = (lax.broadcasted_iota(jnp.int32, (Q_TILE, Q_TILE), 0)
               <= lax.broadcasted_iota(jnp.int32, (Q_TILE, Q_TILE), 1))
    nt = (((1,), (1,)), ((), ()))

    def with_decay(x, dec, j, ones_first):
        base = HEAD_DIM * (1 - j)
        start = base if ones_first else base + DECAY_PARTS
        aug = jnp.where(lane >= start, jnp.where(lane < start + DECAY_PARTS, one, dec), dec)
        own = (lane < HEAD_DIM) if j == 0 else (lane >= HEAD_DIM)
        return jnp.where(own, x, aug)

    n_tiles = s_len // Q_TILE
    n_pairs = q_ref.shape[1]
    rows = lambda ref, p, i: ref[0, p, i * Q_TILE:(i + 1) * Q_TILE, :]
    k_heads = {(p, j): jnp.concatenate([with_decay(rows(k_ref, p, i), rows(dec_ref, p, i), j, ones_first=True)
                                        for i in range(n_tiles)], axis=0)
               for p in range(n_pairs) for j in range(HEADS_PER_TILE)}

    def scores(p, i, j):
        qh = with_decay(rows(q_ref, p, i), rows(dec_ref, p, i), j, ones_first=False)
        return lax.dot_general(k_heads[p, j][:(i + 1) * Q_TILE], qh, nt, preferred_element_type=F32)

    def attend(p, i, j, st):
        q0, q1 = i * Q_TILE, (i + 1) * Q_TILE
        diag = jnp.where(visible, st[q0:], -jnp.inf)
        past = st[:q0] if i else None
        if not bounded:
            m = jnp.max(diag, axis=0, keepdims=True)
            if i:
                m = jnp.maximum(m, jnp.max(past, axis=0, keepdims=True))
                past = past - m
            diag = diag - m
        pt = jnp.exp2(diag).astype(BF16)
        if i:
            pt = jnp.concatenate([jnp.exp2(past).astype(BF16), pt], axis=0)
        v0 = (p * HEADS_PER_TILE + j) * V_ROWS
        acc = _dot(vt_ref[0, v0:v0 + V_ROWS, :q1], pt)
        return acc[:HEAD_DIM] * (1.0 / acc[HEAD_DIM:HEAD_DIM + 1])

    units = [(p, i, j) for p in range(n_pairs) for i in range(n_tiles) for j in range(HEADS_PER_TILE)]
    pending = [scores(*u) for u in units[:SCORE_LOOKAHEAD]]
    outs = {}
    for n, (p, i, j) in enumerate(units):
        if n + SCORE_LOOKAHEAD < len(units):
            pending.append(scores(*units[n + SCORE_LOOKAHEAD]))
        outs[j] = attend(p, i, j, pending.pop(0))
        if len(outs) == HEADS_PER_TILE:
            o_t = jnp.concatenate([outs[h] for h in range(HEADS_PER_TILE)], axis=0)
            o_ref[0, p, i * Q_TILE:(i + 1) * Q_TILE, :] = o_t.T.astype(BF16)
            outs = {}


def _fox_attn(q, k, dec, vt, *, bounded):
    b, n_pairs, s, _ = q.shape
    pp = ATTN_PAIRS_PER_STEP_BOUNDED if bounded else ATTN_PAIRS_PER_STEP
    tile = pl.BlockSpec((1, pp, s, LANES), lambda i, j: (i, j, 0, 0))
    return pl.pallas_call(
        functools.partial(_fox_attn_kernel, bounded=bounded),
        grid=(b, n_pairs // pp),
        in_specs=[tile, tile, tile,
                  pl.BlockSpec((1, pp * HEADS_PER_TILE * V_ROWS, s), lambda i, j: (i, j, 0))],
        out_specs=tile,
        out_shape=jax.ShapeDtypeStruct(q.shape, BF16),
        compiler_params=_params(("parallel", "parallel")),
        name="fox_attention_bounded" if bounded else "fox_attention",
    )(q, k, dec, vt)


def _logit_bound(qg, kg):
    return (HEAD_DIM ** 0.5 * LOG2E * 1.02) * jnp.max(jnp.abs(qg)) * jnp.max(jnp.abs(kg))


def kernel(x, norm_mix_g, norm_ffn_g, a_w_in, a_ln_g, a_ln_b, a_w_s, a_b_s, a_w_out,
           b_w_in, b_f_bias, b_q_norm_g, b_k_norm_g, b_w_out,
           ffn_w_gate, ffn_w_up, ffn_w_down):
    bsz, seq, d = x.shape
    row = lambda p: p.reshape(1, -1).astype(F32)
    bf = lambda w: w.astype(BF16)

    b_s_tile = jnp.repeat(a_b_s[0].T, LANES, axis=1)
    later = (ffn_w_gate, ffn_w_up, ffn_w_down, b_w_out)
    w_in_t = jnp.swapaxes(b_w_in[0], 0, 1)
    h, *later_bf, w_in = _gmlp_layer(x.reshape(bsz * seq, d), row(norm_mix_g[0]), bf(a_w_in[0]),
                                     row(a_ln_g[0]), row(a_ln_b[0]), a_w_s[0], b_s_tile, bf(a_w_out[0]),
                                     [w.reshape(-1, w.shape[-1]) for w in later], w_in_t, 4 * d)
    *ffn_w, w_o = (w2.reshape(w.shape) for w2, w in zip(later_bf, later))
    h = _ffn_layer(h, 0, row(norm_ffn_g[0]), *ffn_w)

    fb_pad = jnp.pad(b_f_bias[0], (0, LANES - N_HEADS)).reshape(1, LANES)
    qg_tile = jnp.tile(b_q_norm_g[0], HEADS_PER_TILE).reshape(1, LANES)
    kg_tile = jnp.tile(b_k_norm_g[0], HEADS_PER_TILE).reshape(1, LANES)
    q, k, dec, vt, gate = _fox_proj(h.reshape(bsz, seq, d), row(norm_mix_g[1]),
                                    w_in, w_in_t, fb_pad, qg_tile, kg_tile)
    attn = lax.cond(_logit_bound(b_q_norm_g[0], b_k_norm_g[0]) <= LOGIT_LIMIT,
                    functools.partial(_fox_attn, bounded=True),
                    functools.partial(_fox_attn, bounded=False), q, k, dec, vt)
    h = _ffn_layer(h, 1, row(norm_ffn_g[1]), *ffn_w, attn=attn, gate=gate, w_o=w_o[0])
    return h.reshape(bsz, seq, d)
```

```python
import functools
import math

import jax
import jax.numpy as jnp
from jax import lax
from jax.experimental import pallas as pl
from jax.experimental.pallas import tpu as pltpu

F32 = jnp.float32
BF16 = jnp.bfloat16

EPS = 1e-6
CHUNK = 64
GMLP_BLOCK = 128
GMLP_GROUPS = 8
N_HEADS = 16
HEAD_DIM = 64
LANES = 128
SUBLANES = 8
BF16_SUBLANES = 16
HEADS_PER_TILE = LANES // HEAD_DIM
V_ROWS = HEAD_DIM + BF16_SUBLANES
LOG2E = math.log2(math.e)

ROW_TILE = 1024
PROJ_SUB_TILE = 256
GMLP_ROW_TILE = 1024
GMLP_SUB_TILE = 256
FFN_ROW_TILE = 1024
FFN_SUB_TILE = 256
Q_TILE = 256
SCORE_LOOKAHEAD = 4
ATTN_PAIRS_PER_STEP = 2
ATTN_PAIRS_PER_STEP_BOUNDED = 4
VMEM_LIMIT_BYTES = 56 * 1024 * 1024
LOGIT_LIMIT = 64.0


def _const_spec(shape):
    nd = len(shape)
    return pl.BlockSpec(shape, lambda *_: (0,) * nd, pipeline_mode=pl.Buffered(1))


def _params(semantics):
    return pltpu.CompilerParams(dimension_semantics=semantics,
                                vmem_limit_bytes=VMEM_LIMIT_BYTES)


def _rms_norm(x, g):
    ms = jnp.mean(x * x, axis=-1, keepdims=True)
    return x * lax.rsqrt(ms + EPS) * g


def _dot(a, b):
    return jnp.dot(a, b, preferred_element_type=F32)


def _gelu_exact(x):
    return 0.5 * x * (1.0 + lax.erf(x * math.sqrt(0.5)))


def _split3(x):
    hi = x.astype(BF16).astype(F32)
    r = x - hi
    mid = r.astype(BF16).astype(F32)
    lo = (r - mid).astype(BF16).astype(F32)
    return hi, mid, lo


def _gmlp_kernel(x_ref, g_ref, win_ref, lng_ref, lnb_ref, ws_ref, bs_ref, wout_ref, *rest, n_cast):
    cast_srcs, wt_ref, o_ref = rest[:n_cast], rest[n_cast], rest[n_cast + 1]
    cast_dsts, wt_t_ref = rest[n_cast + 2:-1], rest[-1]
    for src_ref, dst_ref in zip(cast_srcs, cast_dsts):
        dst_ref[...] = src_ref[...].astype(dst_ref.dtype)
    wt_t_ref[...] = wt_ref[...].T.astype(wt_t_ref.dtype)

    d = x_ref.shape[-1]
    t_chunk = lax.broadcasted_iota(jnp.int32, (GMLP_BLOCK, GMLP_BLOCK), 0) // CHUNK
    s_chunk = lax.broadcasted_iota(jnp.int32, (GMLP_BLOCK, GMLP_BLOCK), 1) // CHUNK
    visible = s_chunk <= t_chunk
    ws = [jnp.where(visible, ws_ref[g], 0.0).astype(BF16) for g in range(GMLP_GROUPS)]
    bias = bs_ref[...]

    def expand(r):
        x = x_ref[r * GMLP_SUB_TILE:(r + 1) * GMLP_SUB_TILE, :]
        return _dot(_rms_norm(x, g_ref[...]).astype(BF16), win_ref[...])

    def mix(z):
        z = _gelu_exact(z)
        u = z[:, :d]
        v = z[:, d:]
        mu = jnp.mean(v, axis=-1, keepdims=True)
        vc = v - mu
        var = jnp.mean(vc * vc, axis=-1, keepdims=True)
        vn = (vc * lax.rsqrt(var + EPS) * lng_ref[...] + lnb_ref[...]).astype(BF16)
        blocks = []
        for n in range(GMLP_SUB_TILE // GMLP_BLOCK):
            vb = vn[n * GMLP_BLOCK:(n + 1) * GMLP_BLOCK]
            cols = [_dot(ws[g], vb[:, g * LANES:(g + 1) * LANES]) for g in range(GMLP_GROUPS)]
            blocks.append(jnp.concatenate(cols, axis=1) + bias)
        return (u * jnp.concatenate(blocks, axis=0)).astype(BF16)

    def project(r, gated):
        rows = slice(r * GMLP_SUB_TILE, (r + 1) * GMLP_SUB_TILE)
        o_ref[rows, :] = x_ref[rows, :] + _dot(gated, wout_ref[...])

    n_sub = x_ref.shape[0] // GMLP_SUB_TILE
    z = {r: expand(r) for r in range(min(2, n_sub))}
    gated = {0: mix(z.pop(0))}
    for r in range(n_sub):
        if r + 2 < n_sub:
            z[r + 2] = expand(r + 2)
        if r + 1 < n_sub:
            gated[r + 1] = mix(z.pop(r + 1))
        project(r, gated.pop(r))


def _gmlp_layer(x2d, g, w_in, ln_g, ln_b, w_s, b_s_tile, w_out, to_cast, wt, wt_rows):
    t, d = x2d.shape
    steps = t // GMLP_ROW_TILE
    row = pl.BlockSpec((GMLP_ROW_TILE, d), lambda i: (i, 0))
    cast_specs = [pl.BlockSpec((w.shape[0] // steps, w.shape[1]), lambda i: (i, 0)) for w in to_cast]
    blk = wt_rows // steps
    return pl.pallas_call(
        functools.partial(_gmlp_kernel, n_cast=len(to_cast)),
        grid=(steps,),
        in_specs=[row, _const_spec(g.shape), _const_spec(w_in.shape), _const_spec(ln_g.shape),
                  _const_spec(ln_b.shape), _const_spec(w_s.shape), _const_spec(b_s_tile.shape),
                  _const_spec(w_out.shape)] + cast_specs
                 + [pl.BlockSpec((blk, wt.shape[1]), lambda i: (i, 0))],
        out_specs=[row] + cast_specs + [pl.BlockSpec((wt.shape[1], blk), lambda i: (0, i))],
        out_shape=[jax.ShapeDtypeStruct((t, d), F32)]
                  + [jax.ShapeDtypeStruct(w.shape, BF16) for w in to_cast]
                  + [jax.ShapeDtypeStruct((wt.shape[1], wt_rows), BF16)],
        compiler_params=_params(("parallel",)),
        name="gmlp_mixer",
    )(x2d, g, w_in, ln_g, ln_b, w_s, b_s_tile, w_out, *to_cast, wt)


def _swiglu_pipelined(residual, g_ref, wg_ref, wu_ref, wd_ref, o_ref):
    def expand(h):
        hn = _rms_norm(h, g_ref[...]).astype(BF16)
        return _dot(hn, wg_ref[...]), _dot(hn, wu_ref[...])

    def contract(r, h, a, b):
        act = (a * jax.nn.sigmoid(a) * b).astype(BF16)
        o_ref[r * FFN_SUB_TILE:(r + 1) * FFN_SUB_TILE, :] = h + _dot(act, wd_ref[...])

    n_sub = o_ref.shape[0] // FFN_SUB_TILE
    h = residual(0)
    a, b = expand(h)
    for r in range(n_sub):
        h_next = residual(r + 1) if r + 1 < n_sub else None
        contract(r, h, a, b)
        if h_next is not None:
            h = h_next
            a, b = expand(h)


def _ffn_kernel(h_ref, g_ref, wg_ref, wu_ref, wd_ref, o_ref):
    residual = lambda r: h_ref[r * FFN_SUB_TILE:(r + 1) * FFN_SUB_TILE, :]
    _swiglu_pipelined(residual, g_ref, wg_ref, wu_ref, wd_ref, o_ref)


def _load_head_pairs(ref, rows):
    return jnp.concatenate([ref[0, p, rows, :] for p in range(ref.shape[1])], axis=1).astype(F32)


def _proj_ffn_kernel(h_ref, a_ref, gate_ref, wo_ref, g_ref, wg_ref, wu_ref, wd_ref, o_ref):
    def residual(r):
        rows = slice(r * FFN_SUB_TILE, (r + 1) * FFN_SUB_TILE)
        a = (_load_head_pairs(a_ref, rows) * _load_head_pairs(gate_ref, rows)).astype(BF16)
        return h_ref[rows, :] + _dot(a, wo_ref[...])
    _swiglu_pipelined(residual, g_ref, wg_ref, wu_ref, wd_ref, o_ref)


def _layer_spec(stacked, layer):
    nd = stacked.ndim - 1
    return pl.BlockSpec((None,) + stacked.shape[1:], lambda *_: (layer,) + (0,) * nd,
                        pipeline_mode=pl.Buffered(1))


def _ffn_layer(h2d, layer, g, w_gate, w_up, w_down, attn=None, gate=None, w_o=None):
    t, d = h2d.shape
    tm = FFN_ROW_TILE
    row = pl.BlockSpec((tm, d), lambda i: (i, 0))
    weights = [g, w_gate, w_up, w_down]
    weight_specs = [_const_spec(g.shape)] + [_layer_spec(w, layer) for w in weights[1:]]
    if attn is None:
        kernel, args, specs, name = _ffn_kernel, [h2d], [row], "swiglu_ffn"
    else:
        tiles_per_seq = attn.shape[2] // tm
        pairs = pl.BlockSpec((1, attn.shape[1], tm, LANES),
                             lambda i: (i // tiles_per_seq, 0, i % tiles_per_seq, 0))
        kernel, args, specs, name = _proj_ffn_kernel, [h2d, attn, gate, w_o], \
            [row, pairs, pairs, _const_spec(w_o.shape)], "attn_out_swiglu_ffn"
    return pl.pallas_call(
        kernel,
        grid=(t // tm,),
        in_specs=specs + weight_specs,
        out_specs=row,
        out_shape=jax.ShapeDtypeStruct((t, d), F32),
        compiler_params=_params(("parallel",)),
        name=name,
    )(*args, *weights)


DECAY_PARTS = 3


def _head_norm(x, gain, first_half):
    sq = x * x
    ss_a = jnp.sum(jnp.where(first_half, sq, 0.0), axis=-1, keepdims=True)
    ss_b = jnp.sum(jnp.where(first_half, 0.0, sq), axis=-1, keepdims=True)
    inv = lax.rsqrt(jnp.where(first_half, ss_a, ss_b) * (1.0 / HEAD_DIM) + EPS)
    return x * inv * gain


def _store_head_pairs(ref, rows, x, fn=None):
    for p in range(ref.shape[1]):
        tile = x[:, p * LANES:(p + 1) * LANES]
        ref[0, p, rows, :] = (tile if fn is None else fn(tile)).astype(ref.dtype)


def _decay_rows(parts, head):
    tm = parts[0].shape[1]
    sub = lax.broadcasted_iota(jnp.int32, (SUBLANES, tm), 0)
    rows = jnp.zeros((SUBLANES, tm), F32)
    for n, part in enumerate(parts):
        r = part[head:head + 1, :]
        rows = jnp.where(sub == n, r, jnp.where(sub == n + DECAY_PARTS, -r, rows))
    return rows


def _fox_proj_kernel(h_ref, g_ref, win_ref, wf_ref, fb_ref, qg_ref, kg_ref,
                     q_ref, k_ref, dec_ref, vt_ref, gate_ref, carry_ref):
    @pl.when(pl.program_id(1) == 0)
    def _():
        carry_ref[...] = jnp.zeros_like(carry_ref)

    d = h_ref.shape[-1]
    tm = PROJ_SUB_TILE
    wq_ref, wk_ref, wv_ref, wg_ref = (win_ref.at[:, n * d:(n + 1) * d] for n in range(4))
    first_half = lax.broadcasted_iota(jnp.int32, (1, LANES), 1) < HEAD_DIM
    q_gain = qg_ref[...] * (HEAD_DIM ** -0.5 * LOG2E)
    k_gain = kg_ref[...]
    pos = lax.broadcasted_iota(jnp.int32, (N_HEADS, tm), 1)
    gap = jnp.zeros((HEAD_DIM - SUBLANES, tm), F32)
    ones_rows = jnp.ones((BF16_SUBLANES, tm), F32)
    wf_t = jnp.concatenate([wf_ref[...], jnp.zeros((LANES - N_HEADS, d), F32)], axis=0).astype(BF16)
    nt = (((1,), (1,)), ((), ()))

    for r in range(h_ref.shape[1] // tm):
        rows = slice(r * tm, (r + 1) * tm)
        hn = _rms_norm(h_ref[0, rows, :], g_ref[...]).astype(BF16)

        f_logit = lax.dot_general(hn, wf_t, nt, preferred_element_type=F32)
        log_f = jax.nn.log_sigmoid(f_logit + fb_ref[...])
        q = _dot(hn, wq_ref[...])
        k = _dot(hn, wk_ref[...])
        v = _dot(hn, wv_ref[...])
        g = _dot(hn, wg_ref[...])

        cum = log_f.T[:N_HEADS, :]
        shift = 1
        while shift < tm:
            cum = cum + jnp.where(pos >= shift, pltpu.roll(cum, shift, axis=1), 0.0)
            shift *= 2
        cum = cum + carry_ref[...]
        carry_ref[...] = cum[:, tm - 1:tm]

        _store_head_pairs(q_ref, rows, q, lambda t: _head_norm(t, q_gain, first_half))
        parts = _split3(cum * LOG2E)
        for p in range(dec_ref.shape[1]):
            slab = jnp.concatenate([_decay_rows(parts, HEADS_PER_TILE * p + 1), gap,
                                    _decay_rows(parts, HEADS_PER_TILE * p), gap], axis=0)
            dec_ref[0, p, rows, :] = slab.T.astype(BF16)
        _store_head_pairs(k_ref, rows, k, lambda t: _head_norm(t, k_gain, first_half))
        vt = v.T
        vt_ref[0, :, rows] = jnp.concatenate(
            [piece for h in range(N_HEADS) for piece in (vt[h * HEAD_DIM:(h + 1) * HEAD_DIM], ones_rows)],
            axis=0).astype(BF16)
        _store_head_pairs(gate_ref, rows, jax.nn.sigmoid(g))


def _fox_proj(h3d, g, w_in, w_in_t, fb_pad, qg_tile, kg_tile):
    b, s, d = h3d.shape
    row = pl.BlockSpec((1, ROW_TILE, d), lambda i, j: (i, j, 0))
    pairs = pl.BlockSpec((1, d // LANES, ROW_TILE, LANES), lambda i, j: (i, 0, j, 0))
    bf = jax.ShapeDtypeStruct((b, d // LANES, s, LANES), BF16)
    vt_rows = N_HEADS * V_ROWS
    f_block = (w_in_t.shape[0] - N_HEADS) // N_HEADS
    wf_spec = pl.BlockSpec((N_HEADS, d), lambda *_: (f_block, 0), pipeline_mode=pl.Buffered(1))
    consts = (g, w_in, w_in_t, fb_pad, qg_tile, kg_tile)
    return pl.pallas_call(
        _fox_proj_kernel,
        grid=(b, s // ROW_TILE),
        in_specs=[row] + [wf_spec if w is w_in_t else _const_spec(w.shape) for w in consts],
        out_specs=[pairs, pairs, pairs, pl.BlockSpec((1, vt_rows, ROW_TILE), lambda i, j: (i, 0, j)), pairs],
        out_shape=[bf, bf, bf, jax.ShapeDtypeStruct((b, vt_rows, s), BF16), bf],
        scratch_shapes=[pltpu.VMEM((N_HEADS, 1), F32)],
        compiler_params=_params(("parallel", "arbitrary")),
        name="fox_proj",
    )(h3d, *consts)


def _fox_attn_kernel(q_ref, k_ref, dec_ref, vt_ref, o_ref, *, bounded):
    s_len = q_ref.shape[2]
    lane = lax.broadcasted_iota(jnp.int32, (Q_TILE, LANES), 1).astype(F32).astype(BF16)
    one = jnp.ones((Q_TILE, LANES), BF16)
    visible = (lax.broadcasted_iota(jnp.int32, (Q_TILE, Q_TILE), 0)
               <= lax.broadcasted_iota(jnp.int32, (Q_TILE, Q_TILE), 1))
    nt = (((1,), (1,)), ((), ()))

    def with_decay(x, dec, j, ones_first):
        base = HEAD_DIM * (1 - j)
        start = base if ones_first else base + DECAY_PARTS
        aug = jnp.where(lane >= start, jnp.where(lane < start + DECAY_PARTS, one, dec), dec)
        own = (lane < HEAD_DIM) if j == 0 else (lane >= HEAD_DIM)
        return jnp.where(own, x, aug)

    n_tiles = s_len // Q_TILE
    n_pairs = q_ref.shape[1]
    rows = lambda ref, p, i: ref[0, p, i * Q_TILE:(i + 1) * Q_TILE, :]
    k_heads = {(p, j): jnp.concatenate([with_decay(rows(k_ref, p, i), rows(dec_ref, p, i), j, ones_first=True)
                                        for i in range(n_tiles)], axis=0)
               for p in range(n_pairs) for j in range(HEADS_PER_TILE)}

    def scores(p, i, j):
        qh = with_decay(rows(q_ref, p, i), rows(dec_ref, p, i), j, ones_first=False)
        return lax.dot_general(k_heads[p, j][:(i + 1) * Q_TILE], qh, nt, preferred_element_type=F32)

    def attend(p, i, j, st):
        q0, q1 = i * Q_TILE, (i + 1) * Q_TILE
        diag = jnp.where(visible, st[q0:], -jnp.inf)
        past = st[:q0] if i else None
        if not bounded:
            m = jnp.max(diag, axis=0, keepdims=True)
            if i:
                m = jnp.maximum(m, jnp.max(past, axis=0, keepdims=True))
                past = past - m
            diag = diag - m
        pt = jnp.exp2(diag).astype(BF16)
        if i:
            pt = jnp.concatenate([jnp.exp2(past).astype(BF16), pt], axis=0)
        v0 = (p * HEADS_PER_TILE + j) * V_ROWS
        acc = _dot(vt_ref[0, v0:v0 + V_ROWS, :q1], pt)
        return acc[:HEAD_DIM] * (1.0 / acc[HEAD_DIM:HEAD_DIM + 1])

    units = [(p, i, j) for p in range(n_pairs) for i in range(n_tiles) for j in range(HEADS_PER_TILE)]
    pending = [scores(*u) for u in units[:SCORE_LOOKAHEAD]]
    outs = {}
    for n, (p, i, j) in enumerate(units):
        if n + SCORE_LOOKAHEAD < len(units):
            pending.append(scores(*units[n + SCORE_LOOKAHEAD]))
        outs[j] = attend(p, i, j, pending.pop(0))
        if len(outs) == HEADS_PER_TILE:
            o_t = jnp.concatenate([outs[h] for h in range(HEADS_PER_TILE)], axis=0)
            o_ref[0, p, i * Q_TILE:(i + 1) * Q_TILE, :] = o_t.T.astype(BF16)
            outs = {}


def _fox_attn(q, k, dec, vt, *, bounded):
    b, n_pairs, s, _ = q.shape
    pp = ATTN_PAIRS_PER_STEP_BOUNDED if bounded else ATTN_PAIRS_PER_STEP
    tile = pl.BlockSpec((1, pp, s, LANES), lambda i, j: (i, j, 0, 0))
    return pl.pallas_call(
        functools.partial(_fox_attn_kernel, bounded=bounded),
        grid=(b, n_pairs // pp),
        in_specs=[tile, tile, tile,
                  pl.BlockSpec((1, pp * HEADS_PER_TILE * V_ROWS, s), lambda i, j: (i, j, 0))],
        out_specs=tile,
        out_shape=jax.ShapeDtypeStruct(q.shape, BF16),
        compiler_params=_params(("parallel", "parallel")),
        name="fox_attention_bounded" if bounded else "fox_attention",
    )(q, k, dec, vt)


def _logit_bound(qg, kg):
    return (HEAD_DIM ** 0.5 * LOG2E * 1.02) * jnp.max(jnp.abs(qg)) * jnp.max(jnp.abs(kg))


def kernel(x, norm_mix_g, norm_ffn_g, a_w_in, a_ln_g, a_ln_b, a_w_s, a_b_s, a_w_out,
           b_w_in, b_f_bias, b_q_norm_g, b_k_norm_g, b_w_out,
           ffn_w_gate, ffn_w_up, ffn_w_down):
    bsz, seq, d = x.shape
    assert d == N_HEADS * HEAD_DIM == GMLP_GROUPS * LANES
    assert seq % ROW_TILE == 0 and seq % FFN_ROW_TILE == 0 and seq % Q_TILE == 0
    assert (bsz * seq) % GMLP_ROW_TILE == 0
    assert (d // LANES) % ATTN_PAIRS_PER_STEP_BOUNDED == 0
    row = lambda p: p.reshape(1, -1).astype(F32)
    bf = lambda w: w.astype(BF16)

    b_s_tile = jnp.repeat(a_b_s[0].T, LANES, axis=1)
    later = (ffn_w_gate, ffn_w_up, ffn_w_down, b_w_out)
    w_in_t = jnp.swapaxes(b_w_in[0], 0, 1)
    h, *later_bf, w_in = _gmlp_layer(x.reshape(bsz * seq, d), row(norm_mix_g[0]), bf(a_w_in[0]),
                                     row(a_ln_g[0]), row(a_ln_b[0]), a_w_s[0], b_s_tile, bf(a_w_out[0]),
                                     [w.reshape(-1, w.shape[-1]) for w in later], w_in_t, 4 * d)
    *ffn_w, w_o = (w2.reshape(w.shape) for w2, w in zip(later_bf, later))
    h = _ffn_layer(h, 0, row(norm_ffn_g[0]), *ffn_w)

    fb_pad = jnp.pad(b_f_bias[0], (0, LANES - N_HEADS)).reshape(1, LANES)
    qg_tile = jnp.tile(b_q_norm_g[0], HEADS_PER_TILE).reshape(1, LANES)
    kg_tile = jnp.tile(b_k_norm_g[0], HEADS_PER_TILE).reshape(1, LANES)
    q, k, dec, vt, gate = _fox_proj(h.reshape(bsz, seq, d), row(norm_mix_g[1]),
                                    w_in, w_in_t, fb_pad, qg_tile, kg_tile)
    attn = lax.cond(_logit_bound(b_q_norm_g[0], b_k_norm_g[0]) <= LOGIT_LIMIT,
                    functools.partial(_fox_attn, bounded=True),
                    functools.partial(_fox_attn, bounded=False), q, k, dec, vt)
    h = _ffn_layer(h, 1, row(norm_ffn_g[1]), *ffn_w, attn=attn, gate=gate, w_o=w_o[0])
    return h.reshape(bsz, seq, d)
```
